```python
import jax
import jax.numpy as jnp
from jax import lax
import numpy as np

D_MODEL = 1024
BATCH = 2
SEQ = 8192
DEPTH = 1
DEC_BATCH = 128
DEC_SEQ = 4
PAST_LEN = 8192
PAGE_SIZE = 128

HEAD_DIM = 64
NORM_EPS = 1e-6
NEG_INF = -1e30
NSA_HEADS = 8
NSA_KV_HEADS = 2
NSA_GROUP = NSA_HEADS // NSA_KV_HEADS
NSA_WIDTH = NSA_HEADS * HEAD_DIM
KV_WIDTH = NSA_KV_HEADS * HEAD_DIM
CMP_STRIDE = 16
CMP_LEN = 2 * CMP_STRIDE
CMP_HIDDEN = 2 * HEAD_DIM
SEL_BLOCK = 64
SEL_TOPK = 16
WINDOW = 512
Q_BLOCK = 128
FORCE_BONUS = 1e4
ATTN_SCALE = HEAD_DIM ** -0.5
RWKV_HEADS = 8
RWKV_WIDTH = RWKV_HEADS * HEAD_DIM
DECAY_LORA = 64
ICLR_LORA = 64
GATE_LORA = 128
GN_EPS = 64e-5
MIX_WIDTH = NSA_WIDTH + RWKV_WIDTH
NSA_PROJ = NSA_WIDTH + 6 * KV_WIDTH + 3 * NSA_HEADS
RWKV_PROJ = 3 * RWKV_WIDTH + DECAY_LORA + ICLR_LORA + GATE_LORA
IN_PROJ = NSA_PROJ + RWKV_PROJ
PEER_HEADS = 8
PEER_KEYS = 128
PEER_EXPERTS = PEER_KEYS * PEER_KEYS
PEER_KEY_DIM = 256
PEER_HALF = PEER_KEY_DIM // 2
PEER_TOPK = 16
PEER_BLOCK = 128

kernel_name = 'hymba_nsa_rwkv7_peer_step'


def rmsnorm(x, g):
    xf = x.astype(jnp.float32)
    y = xf * lax.rsqrt(jnp.mean(xf * xf, axis=-1, keepdims=True) + NORM_EPS)
    return (y * g.astype(jnp.float32)).astype(x.dtype)


def masked_softmax(s, mask):
    s = jnp.where(mask, s, NEG_INF)
    m = jnp.max(s, axis=-1, keepdims=True)
    e = jnp.where(mask, jnp.exp(s - m), 0.0)
    return e / jnp.maximum(jnp.sum(e, axis=-1, keepdims=True), 1e-30)


def to_groups(q):
    b, n = q.shape[:2]
    return q.reshape(b, n, NSA_KV_HEADS, NSA_GROUP, HEAD_DIM).astype(jnp.float32)


def split_nsa(p):
    b, t = p.shape[:2]
    q = p[..., :NSA_WIDTH].reshape(b, t, NSA_HEADS, HEAD_DIM)
    kv = p[..., NSA_WIDTH:NSA_WIDTH + 6 * KV_WIDTH].reshape(b, t, 6, NSA_KV_HEADS, HEAD_DIM)
    gates = jax.nn.sigmoid(p[..., NSA_WIDTH + 6 * KV_WIDTH:].astype(jnp.float32)).reshape(b, t, NSA_HEADS, 3)
    return q, kv, gates


def compress(rows, w1, w2, pe):
    b, t = rows.shape[:2]
    n_chunks = t // CMP_STRIDE
    ch = rows[:, :n_chunks * CMP_STRIDE].reshape(b, n_chunks, CMP_STRIDE, NSA_KV_HEADS, HEAD_DIM).astype(jnp.float32)
    w1 = w1.astype(jnp.float32)
    hid = (jnp.einsum('bnlgd,lde->bnge', ch[:, :-1], w1[:CMP_STRIDE])
           + jnp.einsum('bnlgd,lde->bnge', ch[:, 1:], w1[CMP_STRIDE:])
           + jnp.einsum('ld,lde->e', pe.astype(jnp.float32), w1))
    return jnp.einsum('bnge,ed->bngd', jax.nn.gelu(hid, approximate=False), w2.astype(jnp.float32))


def cmp_branch(qg, kc, vc, q_pos):
    n_cmp = kc.shape[1]
    end = jnp.arange(n_cmp) * CMP_STRIDE + (CMP_LEN - 1)
    mask = end[None, :] <= q_pos[:, None]
    s = jnp.einsum('bqgrd,bngd->bqgrn', qg, kc) * ATTN_SCALE
    p = masked_softmax(s, mask[None, :, None, None, :])
    return jnp.einsum('bqgrn,bngd->bqgrd', p, vc), p


def select_positions(p_cmp, q_pos, n_sel):
    n_cmp = p_cmp.shape[-1]
    c_start = jnp.arange(n_cmp) * CMP_STRIDE
    s_start = jnp.arange(n_sel) * SEL_BLOCK
    overlap = ((c_start[:, None] < s_start[None, :] + SEL_BLOCK)
               & (c_start[:, None] + CMP_LEN > s_start[None, :])).astype(jnp.float32)
    imp = jnp.einsum('bqgrn,ns->bqgs', p_cmp, overlap)
    blk = jnp.arange(n_sel)[None, :]
    cur = (q_pos // SEL_BLOCK)[:, None]
    valid = (blk <= cur)[None, :, None, :]
    forced = ((blk == 0) | (blk == cur) | (blk == cur - 1))[None, :, None, :]
    score = jnp.where(valid, imp + jnp.where(forced, FORCE_BONUS, 0.0), NEG_INF)
    top, idx = lax.top_k(score, min(SEL_TOPK, n_sel))
    ok = top > 0.5 * NEG_INF
    pos = idx[..., None] * SEL_BLOCK + jnp.arange(SEL_BLOCK)
    mask = ok[..., None] & (pos <= q_pos[None, :, None, None, None])
    b, n, g = idx.shape[:3]
    return pos.reshape(b, n, g, -1), mask.reshape(b, n, g, -1)


def sel_branch(qg, kg, vg, mask):
    s = jnp.einsum('bqgrd,bqgkd->bqgrk', qg, kg.astype(jnp.float32)) * ATTN_SCALE
    p = masked_softmax(s, mask[:, :, :, None, :])
    return jnp.einsum('bqgrk,bqgkd->bqgrd', p, vg.astype(jnp.float32))


def win_branch(qg, kw, vw, q_pos, k_pos):
    mask = ((k_pos[None, :] <= q_pos[:, None]) & (q_pos[:, None] - k_pos[None, :] < WINDOW)
            & (k_pos[None, :] >= 0))
    s = jnp.einsum('bqgrd,bkgd->bqgrk', qg, kw.astype(jnp.float32)) * ATTN_SCALE
    p = masked_softmax(s, mask[None, :, None, None, :])
    return jnp.einsum('bqgrk,bkgd->bqgrd', p, vw.astype(jnp.float32))


def combine(o_c, o_s, o_w, gates):
    b, n = gates.shape[:2]
    g = gates.reshape(b, n, NSA_KV_HEADS, NSA_GROUP, 3, 1)
    o = g[..., 0, :] * o_c + g[..., 1, :] * o_s + g[..., 2, :] * o_w
    return o.reshape(b, n, NSA_WIDTH)


def nsa_prompt(p, nsa_w):
    ck1, ck2, cpe, cv1, cv2, cvpe = nsa_w
    q, kv, gates = split_nsa(p)
    b, t = q.shape[:2]
    kc = compress(kv[:, :, 0], ck1, ck2, cpe)
    vc = compress(kv[:, :, 1], cv1, cv2, cvpe)
    k_sel, v_sel = kv[:, :, 2], kv[:, :, 3]
    pad = ((0, 0), (WINDOW, 0), (0, 0), (0, 0))
    k_wp = jnp.pad(kv[:, :, 4], pad)
    v_wp = jnp.pad(kv[:, :, 5], pad)
    n_sel = t // SEL_BLOCK
    bidx = jnp.arange(b)[:, None, None, None]
    gidx = jnp.arange(NSA_KV_HEADS)[None, None, :, None]

    def block(qb):
        start = qb * Q_BLOCK
        q_pos = start + jnp.arange(Q_BLOCK)
        qg = to_groups(lax.dynamic_slice_in_dim(q, start, Q_BLOCK, axis=1))
        g = lax.dynamic_slice_in_dim(gates, start, Q_BLOCK, axis=1)
        o_c, p_c = cmp_branch(qg, kc, vc, q_pos)
        pos, m = select_positions(p_c, q_pos, n_sel)
        pc = jnp.clip(pos, 0, t - 1)
        o_s = sel_branch(qg, k_sel[bidx, pc, gidx], v_sel[bidx, pc, gidx], m)
        kw = lax.dynamic_slice_in_dim(k_wp, start, WINDOW + Q_BLOCK, axis=1)
        vw = lax.dynamic_slice_in_dim(v_wp, start, WINDOW + Q_BLOCK, axis=1)
        k_pos = start - WINDOW + jnp.arange(WINDOW + Q_BLOCK)
        o_w = win_branch(qg, kw, vw, q_pos, k_pos)
        return combine(o_c, o_s, o_w, g)

    out = lax.map(block, jnp.arange(t // Q_BLOCK))
    out = jnp.moveaxis(out, 0, 1).reshape(b, t, NSA_WIDTH)
    w_keep = min(WINDOW, t)
    return out, kv[:, :, :4], kv[:, t - w_keep:, 4:6]


def nsa_sample(p, cache_nsa, layer, cache_win, page_table, nsa_w):
    ck1, ck2, cpe, cv1, cv2, cvpe = nsa_w
    q, kv, gates = split_nsa(p)
    b, s = q.shape[:2]
    past = cache_nsa[layer, page_table, :, 0:2]
    past = past.reshape(b, -1, 2, NSA_KV_HEADS, HEAD_DIM)
    p_len = past.shape[1]
    kc = compress(jnp.concatenate([past[:, :, 0], kv[:, :, 0].astype(past.dtype)], axis=1), ck1, ck2, cpe)
    vc = compress(jnp.concatenate([past[:, :, 1], kv[:, :, 1].astype(past.dtype)], axis=1), cv1, cv2, cvpe)
    q_pos = p_len + jnp.arange(s)
    qg = to_groups(q)
    o_c, p_c = cmp_branch(qg, kc, vc, q_pos)
    pos, m = select_positions(p_c, q_pos, -(-(p_len + s) // SEL_BLOCK))
    bidx = jnp.arange(b)[:, None, None, None]
    gidx = jnp.arange(NSA_KV_HEADS)[None, None, :, None]
    in_past = (pos < p_len)[..., None]
    pp = jnp.clip(pos, 0, p_len - 1)
    phys = page_table[bidx, pp // PAGE_SIZE]
    off = pp % PAGE_SIZE
    pn = jnp.clip(pos - p_len, 0, s - 1)
    k_g = jnp.where(in_past, cache_nsa[layer, phys, off, 2, gidx], kv[bidx, pn, 2, gidx])
    v_g = jnp.where(in_past, cache_nsa[layer, phys, off, 3, gidx], kv[bidx, pn, 3, gidx])
    o_s = sel_branch(qg, k_g, v_g, m)
    win_rows = jnp.concatenate([cache_win, kv[:, :, 4:6].astype(cache_win.dtype)], axis=1)
    w_buf = cache_win.shape[1]
    k_pos = p_len - w_buf + jnp.arange(w_buf + s)
    o_w = win_branch(qg, win_rows[:, :, 0], win_rows[:, :, 1], q_pos, k_pos)
    return combine(o_c, o_s, o_w, gates), kv[:, :, :4], win_rows[:, s:]


def wkv_scan(r, w, k, v, kk, a, s0):
    xs = tuple(jnp.moveaxis(z, 1, 0) for z in (r, w, k, v, kk, a))

    def step(S, inp):
        r_t, w_t, k_t, v_t, kk_t, a_t = inp
        sa = jnp.einsum('bhvk,bhk->bhv', S, kk_t)
        S = (S * w_t[:, :, None, :] - sa[..., None] * (kk_t * a_t)[:, :, None, :]
             + v_t[..., None] * k_t[:, :, None, :])
        return S, jnp.einsum('bhvk,bhk->bhv', S, r_t)

    S, ys = lax.scan(step, s0.astype(jnp.float32), xs)
    return jnp.moveaxis(ys, 0, 1), S


def rwkv_mixer(p, shift_prev, wkv0, rwkv_w):
    mu, w0, w2, a0, a2, g2, k_k, k_a, r_k, ln_w, ln_b = [z.astype(jnp.float32) for z in rwkv_w]
    pf = p.astype(jnp.float32)
    b, t = p.shape[:2]
    prev = jnp.concatenate([shift_prev[:, None, :].astype(jnp.float32), pf[:, :-1]], axis=1)
    xm = pf + (prev - pf) * mu
    wd = RWKV_WIDTH
    r = xm[..., :wd]
    k = xm[..., wd:2 * wd]
    v = xm[..., 2 * wd:3 * wd]
    o = 3 * wd
    lw = xm[..., o:o + DECAY_LORA]
    la = xm[..., o + DECAY_LORA:o + DECAY_LORA + ICLR_LORA]
    lg = xm[..., o + DECAY_LORA + ICLR_LORA:]
    w = -jax.nn.softplus(-(w0 + jnp.tanh(lw) @ w2)) - 0.5
    decay = jnp.exp(-jnp.exp(w))
    a = jax.nn.sigmoid(a0 + la @ a2)
    g = jax.nn.sigmoid(lg) @ g2
    shp = (b, t, RWKV_HEADS, HEAD_DIM)
    kk = (k * k_k).reshape(shp)
    kk = kk / jnp.maximum(jnp.sqrt(jnp.sum(kk * kk, axis=-1, keepdims=True)), 1e-12)
    k = k * (1.0 + (a - 1.0) * k_a)
    r, k, v, decay, a = r.reshape(shp), k.reshape(shp), v.reshape(shp), decay.reshape(shp), a.reshape(shp)
    y, wkv = wkv_scan(r, decay, k, v, kk, a, wkv0)
    mean = jnp.mean(y, axis=-1, keepdims=True)
    var = jnp.mean(jnp.square(y - mean), axis=-1, keepdims=True)
    yn = ((y - mean) * lax.rsqrt(var + GN_EPS)).reshape(b, t, wd) * ln_w + ln_b
    bonus = (jnp.sum(r * k * r_k, axis=-1, keepdims=True) * v).reshape(b, t, wd)
    return (yn + bonus) * g, p[:, -1], wkv


def peer(h, w_q, sub_keys, u_tab, v_tab):
    b, t, d = h.shape
    n = b * t
    n_pad = -(-n // PEER_BLOCK) * PEER_BLOCK
    hf = jnp.pad(h.reshape(n, d), ((0, n_pad - n), (0, 0))).reshape(-1, PEER_BLOCK, d)
    keys = sub_keys.astype(jnp.float32)

    def block(xb):
        xf = xb.astype(jnp.float32)
        q = (xf @ w_q.astype(jnp.float32)).reshape(PEER_BLOCK, PEER_HEADS, 2, PEER_HALF)
        q = q * lax.rsqrt(jnp.mean(q * q, axis=-1, keepdims=True) + NORM_EPS)
        s = jnp.einsum('nhcd,hckd->nhck', q, keys)
        s1, i1 = lax.top_k(s[:, :, 0], PEER_TOPK)
        s2, i2 = lax.top_k(s[:, :, 1], PEER_TOPK)
        cand = (s1[..., :, None] + s2[..., None, :]).reshape(PEER_BLOCK, PEER_HEADS, PEER_TOPK * PEER_TOPK)
        sc, ci = lax.top_k(cand, PEER_TOPK)
        e = (jnp.take_along_axis(i1, ci // PEER_TOPK, axis=-1) * PEER_KEYS
             + jnp.take_along_axis(i2, ci % PEER_TOPK, axis=-1))
        gate = jax.nn.softmax(sc, axis=-1)
        act = jax.nn.gelu(jnp.einsum('nhkd,nd->nhk', u_tab[e].astype(jnp.float32), xf), approximate=False)
        return jnp.einsum('nhk,nhkd->nd', gate * act, v_tab[e].astype(jnp.float32)).astype(h.dtype)

    out = lax.map(block, hf).reshape(n_pad, d)[:n]
    return out.reshape(b, t, d)


def prompt_mixer(proj, nsa_w, rwkv_w):
    b = proj.shape[0]
    o_n, rows, win = nsa_prompt(proj[..., :NSA_PROJ], nsa_w)
    o_r, shift, wkv = rwkv_mixer(proj[..., NSA_PROJ:], jnp.zeros((b, RWKV_PROJ), proj.dtype),
                                 jnp.zeros((b, RWKV_HEADS, HEAD_DIM, HEAD_DIM), jnp.float32), rwkv_w)
    return jnp.concatenate([o_n, o_r], axis=-1), (rows, win, wkv, shift)


def sample_mixer(proj, cache_nsa, layer, cache_win, wkv0, shift0, page_table, nsa_w, rwkv_w):
    o_n, rows, win = nsa_sample(proj[..., :NSA_PROJ], cache_nsa, layer, cache_win, page_table, nsa_w)
    o_r, shift, wkv = rwkv_mixer(proj[..., NSA_PROJ:], shift0, wkv0, rwkv_w)
    return jnp.concatenate([o_n, o_r], axis=-1), (rows, win, wkv, shift)


def layer_forward(x, c, mixer, mixer_args, w_ada, b_ada, g_pre1, g_post1, g_pre2, g_post2, w_in, w_out, peer_w):
    sh1, sc1, gt1, sh2, sc2, gt2 = jnp.split((jax.nn.silu(c) @ w_ada + b_ada)[:, None, :], 6, axis=-1)
    h = rmsnorm(x, g_pre1) * (1.0 + sc1) + sh1
    mix, states = mixer(h @ w_in, *mixer_args)
    x = x + gt1 * rmsnorm(mix.astype(x.dtype) @ w_out, g_post1)
    h2 = rmsnorm(x, g_pre2) * (1.0 + sc2) + sh2
    x = x + gt2 * rmsnorm(peer(h2, *peer_w), g_post2)
    return x, states


def setup_inputs(seed: int = 0) -> dict:
    key = jax.random.key(seed)
    keys = iter(jax.random.split(key, 48))

    def nrm(shape, scale):
        return jax.random.normal(next(keys), shape, jnp.float32) * scale

    def unif(shape, lo, hi):
        return jax.random.uniform(next(keys), shape, jnp.float32, lo, hi)

    L = DEPTH
    n_pages = PAST_LEN // PAGE_SIZE
    n_used = DEC_BATCH * n_pages
    n_pool = n_used + max(1, n_used // 4)
    w_buf = min(WINDOW, PAST_LEN)
    page_table = jax.random.permutation(next(keys), n_pool)[:n_used].reshape(DEC_BATCH, n_pages).astype(jnp.int32)
    return {
        'x_prompt': nrm((BATCH, SEQ, D_MODEL), 1.0),
        'x_sample': nrm((DEC_BATCH, DEC_SEQ, D_MODEL), 1.0),
        'c_prompt': nrm((BATCH, D_MODEL), 1.0),
        'c_sample': nrm((DEC_BATCH, D_MODEL), 1.0),
        'cache_nsa': nrm((L, n_pool, PAGE_SIZE, 4, NSA_KV_HEADS, HEAD_DIM), 1.0),
        'page_table': page_table,
        'cache_win': nrm((L, DEC_BATCH, w_buf, 2, NSA_KV_HEADS, HEAD_DIM), 1.0),
        'state_wkv': nrm((L, DEC_BATCH, RWKV_HEADS, HEAD_DIM, HEAD_DIM), 0.3),
        'state_shift': nrm((L, DEC_BATCH, RWKV_PROJ), 1.0),
        'w_ada': nrm((L, D_MODEL, 6 * D_MODEL), 0.5 * D_MODEL ** -0.5),
        'b_ada': nrm((L, 6 * D_MODEL), 0.02),
        'norm_pre1': 1.0 + nrm((L, D_MODEL), 0.05),
        'norm_post1': 1.0 + nrm((L, D_MODEL), 0.05),
        'norm_pre2': 1.0 + nrm((L, D_MODEL), 0.05),
        'norm_post2': 1.0 + nrm((L, D_MODEL), 0.05),
        'w_in': nrm((L, D_MODEL, IN_PROJ), D_MODEL ** -0.5),
        'w_out': nrm((L, MIX_WIDTH, D_MODEL), MIX_WIDTH ** -0.5),
        'cmp_k_w1': nrm((L, CMP_LEN, HEAD_DIM, CMP_HIDDEN), (CMP_LEN * HEAD_DIM) ** -0.5),
        'cmp_k_w2': nrm((L, CMP_HIDDEN, HEAD_DIM), CMP_HIDDEN ** -0.5),
        'cmp_k_pe': nrm((L, CMP_LEN, HEAD_DIM), 0.1),
        'cmp_v_w1': nrm((L, CMP_LEN, HEAD_DIM, CMP_HIDDEN), (CMP_LEN * HEAD_DIM) ** -0.5),
        'cmp_v_w2': nrm((L, CMP_HIDDEN, HEAD_DIM), CMP_HIDDEN ** -0.5),
        'cmp_v_pe': nrm((L, CMP_LEN, HEAD_DIM), 0.1),
        'rwkv_mu': unif((L, RWKV_PROJ), 0.0, 1.0),
        'rwkv_w0': unif((L, RWKV_WIDTH), -6.0, -1.0),
        'rwkv_w2': nrm((L, DECAY_LORA, RWKV_WIDTH), 0.3 * DECAY_LORA ** -0.5),
        'rwkv_a0': nrm((L, RWKV_WIDTH), 0.1),
        'rwkv_a2': nrm((L, ICLR_LORA, RWKV_WIDTH), 0.5 * ICLR_LORA ** -0.5),
        'rwkv_g2': nrm((L, GATE_LORA, RWKV_WIDTH), GATE_LORA ** -0.5),
        'rwkv_k_k': 0.85 + nrm((L, RWKV_WIDTH), 0.05),
        'rwkv_k_a': 1.0 + nrm((L, RWKV_WIDTH), 0.05),
        'rwkv_r_k': nrm((L, RWKV_HEADS, HEAD_DIM), 0.1),
        'rwkv_ln_w': 1.0 + nrm((L, RWKV_WIDTH), 0.05),
        'rwkv_ln_b': nrm((L, RWKV_WIDTH), 0.02),
        'peer_w_q': nrm((L, D_MODEL, PEER_HEADS * PEER_KEY_DIM), D_MODEL ** -0.5),
        'peer_sub_keys': nrm((L, PEER_HEADS, 2, PEER_KEYS, PEER_HALF), PEER_HALF ** -0.5),
        'peer_u': nrm((L, PEER_EXPERTS, D_MODEL), D_MODEL ** -0.5),
        'peer_v': nrm((L, PEER_EXPERTS, D_MODEL), D_MODEL ** -0.5),
    }


def reference(x_prompt, x_sample, c_prompt, c_sample, cache_nsa, page_table, cache_win, state_wkv, state_shift,
              w_ada, b_ada, norm_pre1, norm_post1, norm_pre2, norm_post2, w_in, w_out,
              cmp_k_w1, cmp_k_w2, cmp_k_pe, cmp_v_w1, cmp_v_w2, cmp_v_pe,
              rwkv_mu, rwkv_w0, rwkv_w2, rwkv_a0, rwkv_a2, rwkv_g2, rwkv_k_k, rwkv_k_a, rwkv_r_k,
              rwkv_ln_w, rwkv_ln_b, peer_w_q, peer_sub_keys, peer_u, peer_v):
    xp, xs = x_prompt, x_sample
    rows_p, rows_s, win_p, win_s, wkv_p, wkv_s, sh_p, sh_s = [], [], [], [], [], [], [], []
    for l in range(DEPTH):
        nsa_w = (cmp_k_w1[l], cmp_k_w2[l], cmp_k_pe[l], cmp_v_w1[l], cmp_v_w2[l], cmp_v_pe[l])
        rwkv_w = (rwkv_mu[l], rwkv_w0[l], rwkv_w2[l], rwkv_a0[l], rwkv_a2[l], rwkv_g2[l],
                  rwkv_k_k[l], rwkv_k_a[l], rwkv_r_k[l], rwkv_ln_w[l], rwkv_ln_b[l])
        shared = (w_ada[l], b_ada[l], norm_pre1[l], norm_post1[l], norm_pre2[l], norm_post2[l],
                  w_in[l], w_out[l], (peer_w_q[l], peer_sub_keys[l], peer_u[l], peer_v[l]))
        xp, st_p = layer_forward(xp, c_prompt, prompt_mixer, (nsa_w, rwkv_w), *shared)
        xs, st_s = layer_forward(xs, c_sample, sample_mixer,
                                 (cache_nsa, l, cache_win[l], state_wkv[l], state_shift[l], page_table, nsa_w, rwkv_w),
                                 *shared)
        rows_p.append(st_p[0]); win_p.append(st_p[1]); wkv_p.append(st_p[2]); sh_p.append(st_p[3])
        rows_s.append(st_s[0]); win_s.append(st_s[1]); wkv_s.append(st_s[2]); sh_s.append(st_s[3])
    rows_prompt = jnp.stack(rows_p)
    rows_sample = jnp.stack(rows_s)
    win_prompt = jnp.stack(win_p)
    win_sample = jnp.stack(win_s)
    wkv_prompt = jnp.stack(wkv_p)
    wkv_sample = jnp.stack(wkv_s)
    shift_prompt = jnp.stack(sh_p)
    shift_sample = jnp.stack(sh_s)
    return (xp, xs, rows_prompt, rows_sample, win_prompt, win_sample, wkv_prompt, wkv_sample, shift_prompt, shift_sample)
```

```python
import functools

import numpy as np
import jax
import jax.numpy as jnp
from jax import lax
from jax.experimental import pallas as pl
from jax.experimental.pallas import tpu as pltpu

F32 = jnp.float32
BF16 = jnp.bfloat16
HIGHEST = lax.Precision.HIGHEST

LANES = 128
SUBLANES = 8
VMEM_LIMIT_BYTES = 56 * 1024 * 1024

D_MODEL = 1024
HEAD_DIM = 64
NORM_EPS = 1e-6
NEG_INF = -1e30
BELOW_NEG_INF = -3e38
NSA_HEADS = 8
NSA_KV_HEADS = 2
NSA_GROUP = NSA_HEADS // NSA_KV_HEADS
NSA_WIDTH = NSA_HEADS * HEAD_DIM
KV_WIDTH = NSA_KV_HEADS * HEAD_DIM
CMP_STRIDE = 16
CMP_LEN = 32
CMP_HIDDEN = 2 * HEAD_DIM
SEL_BLOCK = 64
SEL_TOPK = 16
WINDOW = 512
Q_BLOCK = 128
FORCE_BONUS = 1e4
ATTN_SCALE = HEAD_DIM ** -0.5
RWKV_HEADS = 8
RWKV_WIDTH = RWKV_HEADS * HEAD_DIM
DECAY_LORA = 64
ICLR_LORA = 64
GATE_LORA = 128
GN_EPS = 64e-5
NSA_PROJ = NSA_WIDTH + 6 * KV_WIDTH + 3 * NSA_HEADS
RWKV_PROJ = 3 * RWKV_WIDTH + DECAY_LORA + ICLR_LORA + GATE_LORA
KV6 = 6 * KV_WIDTH
GATE_ROWS = 32
PEER_HEADS = 8
PEER_KEYS = 128
PEER_HALF = 128
PEER_TOPK = 16
KEY_CHUNK = 128


def _cparams(sem):
    return pltpu.CompilerParams(dimension_semantics=sem, vmem_limit_bytes=VMEM_LIMIT_BYTES)


def _gelu(x):
    return 0.5 * x * (1.0 + lax.erf(x * np.float32(0.7071067811865476)))


def _dot(a, b):
    return jnp.dot(a, b, preferred_element_type=F32)


def _dot_nt(a, b):
    return lax.dot_general(a, b, (((1,), (1,)), ((), ())), preferred_element_type=F32)


def _dot_hi(a, b):
    return jnp.dot(a, b, preferred_element_type=F32, precision=HIGHEST)


def _mod_kernel(c_ref, w_ref, b_ref, o_ref):
    c = c_ref[...]
    o_ref[...] = _dot_hi(c * jax.nn.sigmoid(c), w_ref[...]) + b_ref[...]


def _modulation(c, w_ada, b_ada):
    rows, d = c.shape
    n = w_ada.shape[1]
    tn = 1536
    return pl.pallas_call(
        _mod_kernel,
        grid=(n // tn,),
        in_specs=[pl.BlockSpec((rows, d), lambda j: (0, 0)),
                  pl.BlockSpec((d, tn), lambda j: (0, j)),
                  pl.BlockSpec((1, tn), lambda j: (0, j))],
        out_specs=pl.BlockSpec((rows, tn), lambda j: (0, j)),
        out_shape=jax.ShapeDtypeStruct((rows, n), F32),
        compiler_params=_cparams(("arbitrary",)),
        name="modulation",
    )(c, w_ada, b_ada.reshape(1, n))


def _inproj_kernel(x_ref, g_ref, sc_ref, sh_ref, wqT_ref, wkv_ref, wkvT_ref, wgT_ref, wr_ref,
                   qT_ref, kv_ref, kvb_ref, kvT_ref, gT_ref, rw_ref):
    x = x_ref[...]
    h = x * lax.rsqrt(jnp.mean(x * x, axis=-1, keepdims=True) + NORM_EPS) * g_ref[...]
    h = (h * (1.0 + sc_ref[...]) + sh_ref[...]).astype(BF16)
    qT_ref[...] = _dot_nt(wqT_ref[...], h)
    kv = _dot(h, wkv_ref[...])
    kv_ref[...] = kv
    kvb_ref[...] = kv.astype(BF16)
    kvT_ref[...] = _dot_nt(wkvT_ref[...], h).astype(BF16)
    gT_ref[...] = jax.nn.sigmoid(_dot_nt(wgT_ref[...], h))
    rw_ref[...] = _dot(h, wr_ref[...])


def _mod_specs(mod, tm, tokens_per_row, col):
    if tokens_per_row is None:
        return pl.BlockSpec((tm, D_MODEL), lambda i, *_: (i, col))
    tiles_per_row = tokens_per_row // tm
    return pl.BlockSpec((None, 1, D_MODEL), lambda i, *_: (i // tiles_per_row, 0, col))


def _in_projection(x, g_pre, mod, tokens_per_row, w_in, tm):
    n = x.shape[0]
    wb = w_in.astype(BF16)
    wq = wb[:, :NSA_WIDTH]
    wkv = wb[:, NSA_WIDTH:NSA_WIDTH + KV6]
    wg = jnp.pad(wb[:, NSA_WIDTH + KV6:NSA_PROJ], ((0, 0), (0, GATE_ROWS - 3 * NSA_HEADS)))
    wr = wb[:, NSA_PROJ:]
    full = lambda a: pl.BlockSpec(a.shape, lambda i: (0,) * a.ndim)
    wqT, wkvT, wgT = wq.T, wkv.T, wg.T
    g2 = g_pre.reshape(1, D_MODEL)
    outs = (
        jax.ShapeDtypeStruct((NSA_WIDTH, n), F32),
        jax.ShapeDtypeStruct((n, KV6), F32),
        jax.ShapeDtypeStruct((n, KV6), BF16),
        jax.ShapeDtypeStruct((KV6, n), BF16),
        jax.ShapeDtypeStruct((GATE_ROWS, n), F32),
        jax.ShapeDtypeStruct((n, RWKV_PROJ), F32),
    )
    return pl.pallas_call(
        _inproj_kernel,
        grid=(n // tm,),
        in_specs=[pl.BlockSpec((tm, D_MODEL), lambda i: (i, 0)), full(g2),
                  _mod_specs(mod, tm, tokens_per_row, 1), _mod_specs(mod, tm, tokens_per_row, 0),
                  full(wqT), full(wkv), full(wkvT), full(wgT), full(wr)],
        out_specs=(pl.BlockSpec((NSA_WIDTH, tm), lambda i: (0, i)),
                   pl.BlockSpec((tm, KV6), lambda i: (i, 0)),
                   pl.BlockSpec((tm, KV6), lambda i: (i, 0)),
                   pl.BlockSpec((KV6, tm), lambda i: (0, i)),
                   pl.BlockSpec((GATE_ROWS, tm), lambda i: (0, i)),
                   pl.BlockSpec((tm, RWKV_PROJ), lambda i: (i, 0))),
        out_shape=outs,
        compiler_params=_cparams(("arbitrary",)),
        name="in_projection",
    )(x, g2, mod, mod, wqT, wkv, wkvT, wgT, wr)


def _compress_weights(w1, w2, pe):
    z = jnp.zeros((CMP_STRIDE, HEAD_DIM, CMP_HIDDEN), F32)
    a, b = w1[:CMP_STRIDE], w1[CMP_STRIDE:]
    top = jnp.concatenate([a, z, b, z], axis=-1)
    bot = jnp.concatenate([z, a, z, b], axis=-1)
    wbd = jnp.concatenate([top, bot], axis=1).astype(BF16)
    z2 = jnp.zeros((CMP_HIDDEN, HEAD_DIM), F32)
    w2bd = jnp.concatenate([jnp.concatenate([w2, z2], axis=1),
                            jnp.concatenate([z2, w2], axis=1)], axis=0).astype(BF16)
    return wbd, w2bd, w2bd.T, pe.reshape(1, CMP_LEN * HEAD_DIM), w1.reshape(CMP_LEN * HEAD_DIM, CMP_HIDDEN)


def _compress_bias(pe_ref, w1_ref):
    b = _dot_hi(jnp.broadcast_to(pe_ref[...], (SUBLANES, CMP_LEN * HEAD_DIM)), w1_ref[...])[0:1]
    return jnp.concatenate([b, b], axis=1)


def _compress_hidden(rows_at, n_chunks, wbd_ref, bias):
    acc = jnp.zeros((n_chunks, 4 * CMP_HIDDEN), F32)
    for l in range(CMP_STRIDE):
        acc = acc + _dot(rows_at(l).astype(BF16), wbd_ref[l])
    first, second = acc[:, :2 * CMP_HIDDEN], acc[:, 2 * CMP_HIDDEN:]
    hid = first + pltpu.roll(second, n_chunks - 1, 0) + bias
    return _gelu(hid).astype(BF16)


def _cmp_prompt_kernel(k_ref, v_ref, kwbd_ref, kw2_ref, kpe_ref, kw1_ref, vwbd_ref, vw2T_ref, vpe_ref, vw1_ref,
                       kc_ref, vcT_ref, kb_ref, vb_ref):
    n_chunks = k_ref.shape[0] // CMP_STRIDE
    kb = _compress_bias(kpe_ref, kw1_ref)
    vb = _compress_bias(vpe_ref, vw1_ref)
    kb_ref[...] = kb
    vb_ref[...] = vb
    gk = _compress_hidden(lambda l: k_ref[pl.ds(l, n_chunks, stride=CMP_STRIDE), :], n_chunks, kwbd_ref, kb)
    kc_ref[...] = _dot(gk, kw2_ref[...]).astype(BF16)
    gv = _compress_hidden(lambda l: v_ref[pl.ds(l, n_chunks, stride=CMP_STRIDE), :], n_chunks, vwbd_ref, vb)
    vcT_ref[...] = _dot_nt(vw2T_ref[...], gv).astype(BF16)


def _compress_prompt(kv, batch, seq, kw, vw):
    nc = seq // CMP_STRIDE
    kwbd, kw2, _, kpe, kw1 = kw
    vwbd, _, vw2T, vpe, vw1 = vw
    full = lambda a: pl.BlockSpec(a.shape, lambda b: (0,) * a.ndim)
    return pl.pallas_call(
        _cmp_prompt_kernel,
        grid=(batch,),
        in_specs=[pl.BlockSpec((seq, KV_WIDTH), lambda b: (b, 0)),
                  pl.BlockSpec((seq, KV_WIDTH), lambda b: (b, 1)),
                  full(kwbd), full(kw2), full(kpe), full(kw1), full(vwbd), full(vw2T), full(vpe), full(vw1)],
        out_specs=(pl.BlockSpec((None, nc, KV_WIDTH), lambda b: (b, 0, 0)),
                   pl.BlockSpec((None, KV_WIDTH, nc), lambda b: (b, 0, 0)),
                   pl.BlockSpec((1, 2 * CMP_HIDDEN), lambda b: (0, 0)),
                   pl.BlockSpec((1, 2 * CMP_HIDDEN), lambda b: (0, 0))),
        out_shape=(jax.ShapeDtypeStruct((batch, nc, KV_WIDTH), BF16),
                   jax.ShapeDtypeStruct((batch, KV_WIDTH, nc), BF16),
                   jax.ShapeDtypeStruct((1, 2 * CMP_HIDDEN), F32),
                   jax.ShapeDtypeStruct((1, 2 * CMP_HIDDEN), F32)),
        compiler_params=_cparams(("arbitrary",)),
        name="compress_prompt",
    )(kv, kv, kwbd, kw2, kpe, kw1, vwbd, vw2T, vpe, vw1)


def _overlap_T(n_sel_rows, n_cmp_cols, n_sel, n_cmp):
    s = np.arange(n_sel_rows)[:, None] * SEL_BLOCK
    c = np.arange(n_cmp_cols)[None, :] * CMP_STRIDE
    ov = (c < s + SEL_BLOCK) & (c + CMP_LEN > s)
    ov &= (np.arange(n_sel_rows)[:, None] < n_sel) & (np.arange(n_cmp_cols)[None, :] < n_cmp)
    return jnp.asarray(ov.astype(np.float32))


def _masked_softmax_cols(s, mask):
    s = jnp.where(mask, s, NEG_INF)
    m = jnp.max(s, axis=0, keepdims=True)
    e = jnp.where(mask, jnp.exp(s - m), 0.0)
    return e / jnp.maximum(jnp.sum(e, axis=0, keepdims=True), 1e-30)


def _select_blocks(imp, q_pos, n_sel):
    rows = imp.shape[0]
    blk = lax.broadcasted_iota(jnp.int32, imp.shape, 0)
    cur = jnp.right_shift(q_pos, 6)
    valid = blk <= cur
    forced = (blk == 0) | (blk == cur) | (blk == cur - 1)
    score = jnp.where(valid, imp + jnp.where(forced, FORCE_BONUS, 0.0), NEG_INF)
    score = jnp.where(blk < n_sel, score, BELOW_NEG_INF)
    sel = jnp.zeros(imp.shape, F32)
    for _ in range(min(SEL_TOPK, n_sel)):
        m = jnp.max(score, axis=0, keepdims=True)
        idx = jnp.min(jnp.where(score == m, blk, rows), axis=0, keepdims=True)
        hit = blk == idx
        sel = jnp.where(hit & (m > 0.5 * NEG_INF), 1.0, sel)
        score = jnp.where(hit, BELOW_NEG_INF, score)
    return sel


def _online_update(carry, s, mask, weighted_values):
    m, l, acc = carry
    s = jnp.where(mask, s, NEG_INF)
    m_new = jnp.maximum(m, jnp.max(s, axis=0, keepdims=True))
    alpha = jnp.exp(m - m_new)
    e = jnp.where(mask, jnp.exp(s - m_new), 0.0)
    l = alpha * l + jnp.sum(e, axis=0, keepdims=True)
    acc = alpha * acc + weighted_values(e.astype(BF16))
    return m_new, l, acc


def _online_init(rows, cols):
    return (jnp.full((1, cols), NEG_INF, F32), jnp.zeros((1, cols), F32), jnp.zeros((rows, cols), F32))


def _online_finish(carry):
    _, l, acc = carry
    return acc / jnp.maximum(l, 1e-30)


def _nsa_prompt_kernel(qT_ref, ksel_ref, kwin_ref, vselT_ref, vwinT_ref, kc_ref, vcT_ref, gT_ref, ovT_ref,
                       o_ref, sel_ref):
    qb = pl.program_id(1)
    n_cmp_rows = kc_ref.shape[0]
    n_sel = ovT_ref.shape[0]
    cols = NSA_GROUP * Q_BLOCK
    pos0 = qb * Q_BLOCK
    qT = qT_ref[...] * ATTN_SCALE
    gates = gT_ref[...]
    q_pos1 = pos0 + lax.broadcasted_iota(jnp.int32, (1, Q_BLOCK), 1)
    q_pos = jnp.concatenate([q_pos1] * NSA_GROUP, axis=1)
    key_row = lax.broadcasted_iota(jnp.int32, (KEY_CHUNK, cols), 0)
    zeros_half = jnp.zeros((HEAD_DIM, cols), F32)
    out_rows = []
    for g in range(NSA_KV_HEADS):
        qg = jnp.concatenate([qT[(g * NSA_GROUP + r) * HEAD_DIM:(g * NSA_GROUP + r + 1) * HEAD_DIM, :]
                              for r in range(NSA_GROUP)], axis=1)
        qbd = jnp.concatenate([qg, zeros_half] if g == 0 else [zeros_half, qg], axis=0).astype(BF16)
        lo, hi = g * HEAD_DIM, (g + 1) * HEAD_DIM

        s = _dot(kc_ref[...], qbd)
        n_idx = lax.broadcasted_iota(jnp.int32, s.shape, 0)
        cmask = (n_idx * CMP_STRIDE + (CMP_LEN - 1) <= q_pos) & (n_idx < n_cmp_rows - 1)
        p = _masked_softmax_cols(s, cmask)
        o_c = _dot(vcT_ref[lo:hi, :], p.astype(BF16))
        p_sum = p[:, 0:Q_BLOCK]
        for r in range(1, NSA_GROUP):
            p_sum = p_sum + p[:, r * Q_BLOCK:(r + 1) * Q_BLOCK]
        imp = _dot_hi(ovT_ref[...], p_sum)
        sel = _select_blocks(imp, q_pos1, n_sel)
        sel_ref[...] = jnp.concatenate([sel] * NSA_GROUP, axis=1)

        def sel_step(j, carry):
            k = ksel_ref[pl.ds(pl.multiple_of(j * KEY_CHUNK, KEY_CHUNK), KEY_CHUNK), :]
            sc = _dot(k, qbd)
            blocks_per_chunk = KEY_CHUNK // SEL_BLOCK
            row_a = sel_ref[pl.ds(j * blocks_per_chunk, 1), :]
            row_b = sel_ref[pl.ds(j * blocks_per_chunk + 1, 1), :]
            chosen = jnp.where(key_row < SEL_BLOCK, row_a, row_b) > 0.5
            mask = chosen & (j * KEY_CHUNK + key_row <= q_pos)
            vT = vselT_ref[lo:hi, pl.ds(pl.multiple_of(j * KEY_CHUNK, KEY_CHUNK), KEY_CHUNK)]
            return _online_update(carry, sc, mask, functools.partial(_dot, vT))
        o_s = _online_finish(lax.fori_loop(0, qb + 1, sel_step, _online_init(HEAD_DIM, cols)))

        def win_step(i, carry):
            jj = qb - WINDOW // KEY_CHUNK + i
            jc = jnp.maximum(jj, 0)
            k = kwin_ref[pl.ds(pl.multiple_of(jc * KEY_CHUNK, KEY_CHUNK), KEY_CHUNK), :]
            sc = _dot(k, qbd)
            k_pos = jj * KEY_CHUNK + key_row
            mask = (k_pos <= q_pos) & (q_pos - k_pos < WINDOW) & (k_pos >= 0)
            vT = vwinT_ref[lo:hi, pl.ds(pl.multiple_of(jc * KEY_CHUNK, KEY_CHUNK), KEY_CHUNK)]
            return _online_update(carry, sc, mask, functools.partial(_dot, vT))
        o_w = _online_finish(lax.fori_loop(0, WINDOW // KEY_CHUNK + 1, win_step, _online_init(HEAD_DIM, cols)))

        for r in range(NSA_GROUP):
            h = g * NSA_GROUP + r
            cs = slice(r * Q_BLOCK, (r + 1) * Q_BLOCK)
            out_rows.append(gates[3 * h:3 * h + 1] * o_c[:, cs] + gates[3 * h + 1:3 * h + 2] * o_s[:, cs]
                            + gates[3 * h + 2:3 * h + 3] * o_w[:, cs])
    o_ref[...] = jnp.concatenate(out_rows, axis=0).T


def _nsa_prompt(qT, kvb, kvT, gT, kc, vcT, batch, seq):
    nq = seq // Q_BLOCK
    nc = seq // CMP_STRIDE
    n_sel = seq // SEL_BLOCK
    ovT = _overlap_T(n_sel, nc, n_sel, nc - 1)
    cols = NSA_GROUP * Q_BLOCK
    return pl.pallas_call(
        _nsa_prompt_kernel,
        grid=(batch, nq),
        in_specs=[pl.BlockSpec((NSA_WIDTH, Q_BLOCK), lambda b, q: (0, b * nq + q)),
                  pl.BlockSpec((seq, KV_WIDTH), lambda b, q: (b, 2)),
                  pl.BlockSpec((seq, KV_WIDTH), lambda b, q: (b, 4)),
                  pl.BlockSpec((KV_WIDTH, seq), lambda b, q: (3, b)),
                  pl.BlockSpec((KV_WIDTH, seq), lambda b, q: (5, b)),
                  pl.BlockSpec((None, nc, KV_WIDTH), lambda b, q: (b, 0, 0)),
                  pl.BlockSpec((None, KV_WIDTH, nc), lambda b, q: (b, 0, 0)),
                  pl.BlockSpec((GATE_ROWS, Q_BLOCK), lambda b, q: (0, b * nq + q)),
                  pl.BlockSpec((n_sel, nc), lambda b, q: (0, 0))],
        out_specs=pl.BlockSpec((Q_BLOCK, NSA_WIDTH), lambda b, q: (b * nq + q, 0)),
        out_shape=jax.ShapeDtypeStruct((batch * seq, NSA_WIDTH), F32),
        scratch_shapes=[pltpu.VMEM((n_sel, cols), F32)],
        compiler_params=_cparams(("arbitrary", "arbitrary")),
        name="nsa_prompt",
    )(qT, kvb, kvb, kvT, kvT, kc, vcT, gT, ovT)


SAMPLE_CHUNK = 512
NEW_ROWS = 8
CACHE_PARTS = 4


def _nsa_sample_kernel(dec_seq, n_sel, pt_ref, q_ref, gate_ref, new_ref, win_ref, cache_ref,
                       kwbd_ref, kw2_ref, kb_ref, vwbd_ref, vw2_ref, vb_ref, ovT_ref, fold_ref,
                       oT_ref, buf_ref, sel_ref, sem_ref):
    b = pl.program_id(0)
    nb = pl.num_programs(0)
    n_pages = pt_ref.shape[1]
    page = cache_ref.shape[1]
    p_len = n_pages * page
    n_chunks = p_len // CMP_STRIDE
    slot = b % 2

    def page_copy(seq, i, part, sl):
        return pltpu.make_async_copy(
            cache_ref.at[pt_ref[seq, i], :, pl.ds(part * KV_WIDTH, KV_WIDTH)],
            buf_ref.at[sl, part, pl.ds(i * page, page), :],
            sem_ref.at[sl])

    def start_fetch(seq, sl):
        def body(i, _):
            for part in range(CACHE_PARTS):
                page_copy(seq, i, part, sl).start()
            return 0
        lax.fori_loop(0, n_pages, body, 0)

    @pl.when(b == 0)
    def _():
        start_fetch(0, 0)

    @pl.when(b + 1 < nb)
    def _():
        start_fetch(b + 1, 1 - slot)

    def wait_body(i, _):
        for part in range(CACHE_PARTS):
            page_copy(b, i, part, slot).wait()
        return 0
    lax.fori_loop(0, n_pages, wait_body, 0)

    qrows = (q_ref[...] * ATTN_SCALE).astype(BF16)
    col = lax.broadcasted_iota(jnp.int32, (1, LANES), 1)
    tok = col & (dec_seq - 1)
    q_pos = p_len + tok

    gk = _compress_hidden(lambda l: buf_ref[slot, 0, pl.ds(l, n_chunks, stride=CMP_STRIDE), :], n_chunks,
                          kwbd_ref, kb_ref[...])
    kc = _dot(gk, kw2_ref[...]).astype(BF16)
    gv = _compress_hidden(lambda l: buf_ref[slot, 1, pl.ds(l, n_chunks, stride=CMP_STRIDE), :], n_chunks,
                          vwbd_ref, vb_ref[...])
    vc = _dot(gv, vw2_ref[...]).astype(BF16)

    def tn(v, e):
        return lax.dot_general(v, e, (((0,), (0,)), ((), ())), preferred_element_type=F32)

    s = _dot_nt(kc, qrows)
    n_idx = lax.broadcasted_iota(jnp.int32, s.shape, 0)
    cmask = (n_idx * CMP_STRIDE + (CMP_LEN - 1) <= q_pos) & (n_idx < n_chunks - 1)
    p = _masked_softmax_cols(s, cmask)
    o_c = tn(vc, p.astype(BF16))
    imp = _dot_hi(_dot_hi(ovT_ref[...], p), fold_ref[...])
    sel = _select_blocks(imp, q_pos, n_sel)
    sel_ref[...] = _dot_hi(sel, fold_ref[...].T)

    blocks_per_chunk = SAMPLE_CHUNK // SEL_BLOCK

    def sel_step(j, carry):
        rows = pl.ds(pl.multiple_of(j * SAMPLE_CHUNK, SAMPLE_CHUNK), SAMPLE_CHUNK)
        k = buf_ref[slot, 2, rows, :].astype(BF16)
        v = buf_ref[slot, 3, rows, :].astype(BF16)
        mask = jnp.concatenate([jnp.broadcast_to(sel_ref[pl.ds(j * blocks_per_chunk + i, 1), :], (SEL_BLOCK, LANES))
                                for i in range(blocks_per_chunk)], axis=0) > 0.5
        return _online_update(carry, _dot_nt(k, qrows), mask, functools.partial(tn, v))
    carry = lax.fori_loop(0, p_len // SAMPLE_CHUNK, sel_step, _online_init(KV_WIDTH, LANES))
    new = new_ref[...]
    new_row = lax.broadcasted_iota(jnp.int32, (NEW_ROWS, LANES), 0)
    new_ok = (new_row <= tok) & (new_row < dec_seq)
    new_sel = jnp.broadcast_to(sel_ref[pl.ds(p_len // SEL_BLOCK, 1), :], (NEW_ROWS, LANES)) > 0.5
    k_new = new[:, 2 * KV_WIDTH:3 * KV_WIDTH].astype(BF16)
    v_new = new[:, 3 * KV_WIDTH:4 * KV_WIDTH].astype(BF16)
    carry = _online_update(carry, _dot_nt(k_new, qrows), new_sel & new_ok, functools.partial(tn, v_new))
    o_s = _online_finish(carry)

    w_buf = win_ref.shape[0]
    win = win_ref[...]
    k_pos = p_len - w_buf + lax.broadcasted_iota(jnp.int32, (w_buf, LANES), 0)
    wmask = (q_pos - k_pos < WINDOW) & (k_pos >= 0)
    carry = _online_update(_online_init(KV_WIDTH, LANES), _dot_nt(win[:, :KV_WIDTH].astype(BF16), qrows), wmask,
                           functools.partial(tn, win[:, KV_WIDTH:].astype(BF16)))
    kw_new = new[:, 4 * KV_WIDTH:5 * KV_WIDTH].astype(BF16)
    vw_new = new[:, 5 * KV_WIDTH:6 * KV_WIDTH].astype(BF16)
    carry = _online_update(carry, _dot_nt(kw_new, qrows), new_ok, functools.partial(tn, vw_new))
    o_w = _online_finish(carry)

    gates = gate_ref[...]
    oT_ref[...] = gates[0:1] * o_c + gates[1:2] * o_s + gates[2:3] * o_w


def _nsa_sample(q, gT, kv_new, cache, page_table, cache_win, kw, vw, kbias, vbias, dec_batch, dec_seq):
    n_pages = page_table.shape[1]
    page = cache.shape[1]
    p_len = n_pages * page
    n_chunks = p_len // CMP_STRIDE
    n_sel = -(-(p_len + dec_seq) // SEL_BLOCK)
    sel_rows = -(-n_sel // SUBLANES) * SUBLANES
    w_buf = cache_win.shape[1]
    n_cols = NSA_HEADS * dec_seq
    q5 = q.reshape(dec_batch, dec_seq, NSA_KV_HEADS, NSA_GROUP, HEAD_DIM).transpose(0, 2, 3, 1, 4)
    eye = jnp.eye(NSA_KV_HEADS, dtype=F32)
    qrows = (q5[:, :, :, :, None, :] * eye[None, :, None, None, :, None]).reshape(dec_batch, n_cols, KV_WIDTH)
    qrows = jnp.pad(qrows, ((0, 0), (0, LANES - n_cols), (0, 0)))
    g4 = gT[:3 * NSA_HEADS].reshape(NSA_HEADS, 3, dec_batch, dec_seq).transpose(2, 1, 0, 3).reshape(dec_batch, 3, n_cols)
    gexp = jnp.pad(g4, ((0, 0), (0, SUBLANES - 3), (0, LANES - n_cols)))
    new = jnp.pad(kv_new.reshape(dec_batch, dec_seq, KV6), ((0, 0), (0, NEW_ROWS - dec_seq), (0, 0)))
    win = cache_win.reshape(dec_batch, w_buf, 2 * KV_WIDTH)
    ovT = _overlap_T(sel_rows, n_chunks, n_sel, n_chunks - 1)
    fold = np.zeros((LANES, LANES), np.float32)
    for g in range(NSA_KV_HEADS):
        for r in range(NSA_GROUP):
            for s in range(dec_seq):
                fold[(g * NSA_GROUP + r) * dec_seq + s, g * dec_seq + s] = 1.0
    fold = jnp.asarray(fold)
    kwbd, kw2, _, _, _ = kw
    vwbd, vw2, _, _, _ = vw
    full = lambda a: pl.BlockSpec(a.shape, lambda b, pt: (0,) * a.ndim)
    per_seq = lambda a: pl.BlockSpec((None,) + a.shape[1:], lambda b, pt: (b,) + (0,) * (a.ndim - 1))
    grid_spec = pltpu.PrefetchScalarGridSpec(
        num_scalar_prefetch=1,
        grid=(dec_batch,),
        in_specs=[per_seq(qrows), per_seq(gexp), per_seq(new), per_seq(win),
                  pl.BlockSpec(memory_space=pl.ANY),
                  full(kwbd), full(kw2), full(kbias), full(vwbd), full(vw2), full(vbias), full(ovT), full(fold)],
        out_specs=pl.BlockSpec((None, KV_WIDTH, LANES), lambda b, pt: (b, 0, 0)),
        scratch_shapes=[pltpu.VMEM((2, CACHE_PARTS, p_len, KV_WIDTH), F32),
                        pltpu.VMEM((sel_rows, LANES), F32),
                        pltpu.SemaphoreType.DMA((2,))])
    oT = pl.pallas_call(
        functools.partial(_nsa_sample_kernel, dec_seq, n_sel),
        grid_spec=grid_spec,
        out_shape=jax.ShapeDtypeStruct((dec_batch, KV_WIDTH, LANES), F32),
        compiler_params=_cparams(("arbitrary",)),
        name="nsa_sample",
    )(page_table, qrows, gexp, new, win, cache, kwbd, kw2, kbias, vwbd, vw2, vbias, ovT, fold)
    o = oT[:, :, :n_cols].reshape(dec_batch, NSA_KV_HEADS, HEAD_DIM, NSA_KV_HEADS, NSA_GROUP, dec_seq)
    o = jnp.stack([o[:, g, :, g] for g in range(NSA_KV_HEADS)], axis=1)
    return o.transpose(0, 4, 1, 3, 2).reshape(dec_batch * dec_seq, NSA_WIDTH)


def _head_ones(width):
    h = np.arange(width) // HEAD_DIM
    return (h[:, None] == h[None, :]).astype(np.float32)


def _rwkv_pre_kernel(seq, use_carry, rw_ref, shift_ref, mu_ref, w0_ref, a0_ref, kk_ref, ka_ref, rk_ref,
                     w2_ref, a2_ref, g2_ref, ones_ref,
                     r_ref, w_ref, k_ref, kkn_ref, b_ref, v_ref, g_ref, bonus_ref, carry_ref):
    i = pl.program_id(0)
    x = rw_ref[...]
    tm = x.shape[0]
    row = lax.broadcasted_iota(jnp.int32, (tm, 1), 0)
    prev = pltpu.roll(x, 1, 0)
    if use_carry:
        @pl.when(i == 0)
        def _():
            carry_ref[...] = jnp.zeros(carry_ref.shape, F32)
        prev = jnp.where(row == 0, carry_ref[...], prev)
        carry_ref[...] = x[tm - 1:tm]
    prev = jnp.where(((i * tm + row) & (seq - 1)) == 0, shift_ref[...], prev)
    xm = x + (prev - x) * mu_ref[...]
    wd = RWKV_WIDTH
    r, k, v = xm[:, :wd], xm[:, wd:2 * wd], xm[:, 2 * wd:3 * wd]
    lwa = xm[:, 3 * wd:3 * wd + DECAY_LORA + ICLR_LORA]
    lg = xm[:, 3 * wd + DECAY_LORA + ICLR_LORA:]
    w = -jax.nn.softplus(-(w0_ref[...] + _dot_hi(jnp.tanh(lwa), w2_ref[...]))) - 0.5
    a = jax.nn.sigmoid(a0_ref[...] + _dot_hi(lwa, a2_ref[...]))
    ones = ones_ref[...]
    kk = k * kk_ref[...]
    kk = kk / jnp.maximum(jnp.sqrt(_dot_hi(kk * kk, ones)), 1e-12)
    k2 = k * (1.0 + (a - 1.0) * ka_ref[...])
    r_ref[...] = r
    w_ref[...] = jnp.exp(-jnp.exp(w))
    k_ref[...] = k2
    kkn_ref[...] = kk
    b_ref[...] = kk * a
    v_ref[...] = v
    g_ref[...] = _dot_hi(jax.nn.sigmoid(lg), g2_ref[...])
    bonus_ref[...] = _dot_hi(r * k2 * rk_ref[...], ones) * v


def _rwkv_pre(rw, shift, seq, tm, rwkv_w):
    mu, w0, w2, a0, a2, g2, k_k, k_a, r_k, _, _ = rwkv_w
    n = rw.shape[0]
    use_carry = seq >= tm
    row1 = lambda a: a.reshape(1, -1)
    z = jnp.zeros((DECAY_LORA, RWKV_WIDTH), F32)
    w2p = jnp.concatenate([w2, z], axis=0)
    a2p = jnp.concatenate([z, a2], axis=0)
    ones = jnp.asarray(_head_ones(RWKV_WIDTH))
    consts = [row1(mu), row1(w0), row1(a0), row1(k_k), row1(k_a), row1(r_k), w2p, a2p, g2, ones]
    full = lambda a: pl.BlockSpec(a.shape, lambda i: (0,) * a.ndim)
    if use_carry:
        shift_spec = pl.BlockSpec((None, 1, RWKV_PROJ), lambda i: (i // (seq // tm), 0, 0))
    else:
        shift_spec = pl.BlockSpec((tm, RWKV_PROJ), lambda i: (i, 0))
    tile = pl.BlockSpec((tm, RWKV_WIDTH), lambda i: (i, 0))
    return pl.pallas_call(
        functools.partial(_rwkv_pre_kernel, seq, use_carry),
        grid=(n // tm,),
        in_specs=[pl.BlockSpec((tm, RWKV_PROJ), lambda i: (i, 0)), shift_spec] + [full(c) for c in consts],
        out_specs=(tile,) * 8,
        out_shape=(jax.ShapeDtypeStruct((n, RWKV_WIDTH), F32),) * 8,
        scratch_shapes=[pltpu.VMEM((1, RWKV_PROJ), F32)],
        compiler_params=_cparams(("arbitrary",)),
        name="rwkv_pre",
    )(rw, shift, *consts)


SCAN_SEQS = 2


def _split3(x):
    hi = x.astype(BF16)
    r1 = x - hi.astype(F32)
    mid = r1.astype(BF16)
    lo = (r1 - mid.astype(F32)).astype(BF16)
    return hi, mid, lo


def _dot3(parts, w):
    return _dot(parts[0], w) + _dot(parts[1], w) + _dot(parts[2], w)


def _head_sums(p, ones_half):
    half = p.shape[1] // 2
    return jnp.concatenate([_dot3(_split3(p[:, :half]), ones_half),
                            _dot3(_split3(p[:, half:]), ones_half)], axis=1)


def _scan_kernel(tc, r_ref, w_ref, k_ref, kk_ref, b_ref, vts_ref, s0_ref, ones_ref, wv_ref, wy_ref,
                 y_ref, s_ref):
    tb = pl.program_id(1)
    n_blocks = vts_ref.shape[1]

    @pl.when(tb == 0)
    def _():
        s_ref[...] = s0_ref[...]

    ones_half = ones_ref[...]
    rows = (HEAD_DIM, RWKV_WIDTH)

    def block(blk, _):
        vparts = [_split3(vts_ref[s, blk]) for s in range(SCAN_SEQS)]

        def step(tp, yacc):
            t = blk * tc + tp
            wv = wv_ref[tp]
            wy = wy_ref[tp]
            out = []
            for s in range(SCAN_SEQS):
                bc = lambda ref: jnp.broadcast_to(ref[s, pl.ds(t, 1), :], rows)
                st = s_ref[s]
                sa = _head_sums(st * bc(kk_ref), ones_half)
                vcol = _dot3(vparts[s], wv)
                st = st * bc(w_ref) - sa * bc(b_ref) + vcol * bc(k_ref)
                s_ref[s] = st
                out.append(yacc[s] + _dot3(_split3(st * bc(r_ref)), wy))
            return tuple(out)

        yacc = lax.fori_loop(0, tc, step, tuple(jnp.zeros((HEAD_DIM, LANES), F32) for _ in range(SCAN_SEQS)))
        for s in range(SCAN_SEQS):
            y_ref[s, blk] = yacc[s]
        return 0

    lax.fori_loop(0, n_blocks, block, 0)


def _scan_consts(tc):
    wv = np.zeros((tc, LANES, RWKV_WIDTH), np.float32)
    for tp in range(tc):
        for h in range(RWKV_HEADS):
            wv[tp, tp * RWKV_HEADS + h, h * HEAD_DIM:(h + 1) * HEAD_DIM] = 1.0
    wy = np.transpose(wv, (0, 2, 1))
    return (jnp.asarray(_head_ones(RWKV_WIDTH // 2)).astype(BF16), jnp.asarray(wv).astype(BF16),
            jnp.asarray(wy).astype(BF16))


def _rwkv_scan(r, w, k, kk, b, v, s0, batch, seq, tt):
    tc = min(LANES // RWKV_HEADS, seq)
    nb = seq // tc
    to3 = lambda a: a.reshape(batch, seq, RWKV_WIDTH)
    vts = v.reshape(batch, nb, tc, RWKV_HEADS, HEAD_DIM).transpose(0, 1, 4, 2, 3).reshape(batch, nb, HEAD_DIM, tc * RWKV_HEADS)
    vts = jnp.pad(vts, ((0, 0), (0, 0), (0, 0), (0, LANES - tc * RWKV_HEADS)))
    s0l = s0.transpose(0, 2, 1, 3).reshape(batch, HEAD_DIM, RWKV_WIDTH)
    ones_half, wv, wy = _scan_consts(tc)
    full = lambda a: pl.BlockSpec(a.shape, lambda p, t: (0,) * a.ndim)
    seq_spec = pl.BlockSpec((SCAN_SEQS, tt, RWKV_WIDTH), lambda p, t: (p, t, 0))
    blk_spec = pl.BlockSpec((SCAN_SEQS, tt // tc, HEAD_DIM, LANES), lambda p, t: (p, t, 0, 0))
    st_spec = pl.BlockSpec((SCAN_SEQS, HEAD_DIM, RWKV_WIDTH), lambda p, t: (p, 0, 0))
    y4, s_out = pl.pallas_call(
        functools.partial(_scan_kernel, tc),
        grid=(batch // SCAN_SEQS, seq // tt),
        in_specs=[seq_spec] * 5 + [blk_spec, st_spec, full(ones_half), full(wv), full(wy)],
        out_specs=(blk_spec, st_spec),
        out_shape=(jax.ShapeDtypeStruct((batch, nb, HEAD_DIM, LANES), F32),
                   jax.ShapeDtypeStruct((batch, HEAD_DIM, RWKV_WIDTH), F32)),
        compiler_params=_cparams(("arbitrary", "arbitrary")),
        name="rwkv_scan",
    )(to3(r), to3(w), to3(k), to3(kk), to3(b), vts, s0l, ones_half, wv, wy)
    y = y4[..., :tc * RWKV_HEADS].reshape(batch, nb, HEAD_DIM, tc, RWKV_HEADS).transpose(0, 1, 3, 4, 2)
    s_fin = s_out.reshape(batch, HEAD_DIM, RWKV_HEADS, HEAD_DIM).transpose(0, 2, 1, 3)
    return y.reshape(batch * seq, RWKV_WIDTH), s_fin


def _out_kernel(x_ref, nsa_ref, y_ref, bonus_ref, g_ref, gt_ref, lnw_ref, lnb_ref, ones_ref, wo_ref, gp_ref, o_ref):
    y = y_ref[...]
    ones = ones_ref[...]
    inv = 1.0 / HEAD_DIM
    yc = y - _dot_hi(y, ones) * inv
    var = _dot_hi(yc * yc, ones) * inv
    yn = yc * lax.rsqrt(var + GN_EPS) * lnw_ref[...] + lnb_ref[...]
    mr = ((yn + bonus_ref[...]) * g_ref[...]).astype(BF16)
    z = _dot(nsa_ref[...].astype(BF16), wo_ref[:NSA_WIDTH, :]) + _dot(mr, wo_ref[NSA_WIDTH:, :])
    zn = z * lax.rsqrt(jnp.mean(z * z, axis=-1, keepdims=True) + NORM_EPS) * gp_ref[...]
    o_ref[...] = x_ref[...] + gt_ref[...] * zn


def _mixer_out(x, nsa, y, bonus, g, mod, tokens_per_row, ln_w, ln_b, w_out, g_post, tm):
    n = x.shape[0]
    ones = jnp.asarray(_head_ones(RWKV_WIDTH))
    wo = w_out.astype(BF16)
    row1 = lambda a: a.reshape(1, -1)
    full = lambda a: pl.BlockSpec(a.shape, lambda i: (0,) * a.ndim)
    half = pl.BlockSpec((tm, RWKV_WIDTH), lambda i: (i, 0))
    xspec = pl.BlockSpec((tm, D_MODEL), lambda i: (i, 0))
    lw, lb, gp = row1(ln_w), row1(ln_b), row1(g_post)
    return pl.pallas_call(
        _out_kernel,
        grid=(n // tm,),
        in_specs=[xspec, half, half, half, half, _mod_specs(mod, tm, tokens_per_row, 2),
                  full(lw), full(lb), full(ones), full(wo), full(gp)],
        out_specs=xspec,
        out_shape=jax.ShapeDtypeStruct((n, D_MODEL), F32),
        compiler_params=_cparams(("arbitrary",)),
        name="mixer_out",
    )(x, nsa, y, bonus, g, mod, lw, lb, ones, wo, gp)


def _staircase():
    return [(a, b) for a in range(PEER_TOPK) for b in range(PEER_TOPK) if (a + 1) * (b + 1) <= PEER_TOPK]


def _top16_ranked(s):
    rows = s.shape[0]
    idx_row = lax.broadcasted_iota(jnp.int32, s.shape, 0)
    rank = jnp.full(s.shape, 99.0, F32)
    vals = []
    for it in range(PEER_TOPK):
        m = jnp.max(s, axis=0, keepdims=True)
        idx = jnp.min(jnp.where(s == m, idx_row, rows), axis=0, keepdims=True)
        hit = idx_row == idx
        rank = jnp.where(hit, np.float32(it), rank)
        s = jnp.where(hit, BELOW_NEG_INF, s)
        vals.append(m)
    return rank, vals


def _peer_kernel(x_ref, g2_ref, sc_ref, sh_ref, gt_ref, gp_ref, wqT_ref, keys_ref, u_ref, vT_ref,
                 o_ref, h_ref, a_ref, cnt_ref, b_ref, r2_ref, acc_ref):
    i = pl.program_id(1)
    n = x_ref.shape[0]

    @pl.when(i == 0)
    def _():
        x = x_ref[...]
        h = x * lax.rsqrt(jnp.mean(x * x, axis=-1, keepdims=True) + NORM_EPS) * g2_ref[...]
        h = (h * (1.0 + sc_ref[...]) + sh_ref[...]).astype(BF16)
        h_ref[...] = h
        acc_ref[...] = jnp.zeros(acc_ref.shape, F32)
        stairs = _staircase()
        n_cand = -(-len(stairs) // SUBLANES) * SUBLANES
        cand_row = lax.broadcasted_iota(jnp.int32, (n_cand, n), 0)
        for hd in range(PEER_HEADS):
            ranks, vals = [], []
            for c in range(2):
                q = _dot_nt(wqT_ref[(hd * 2 + c) * PEER_HALF:(hd * 2 + c + 1) * PEER_HALF, :], h)
                q = q * lax.rsqrt(jnp.mean(q * q, axis=0, keepdims=True) + NORM_EPS)
                s = _dot_hi(keys_ref[hd * 2 + c], q)
                rank, top = _top16_ranked(s)
                ranks.append((rank, s))
                vals.append(top)
            cand = jnp.concatenate([vals[0][a] + vals[1][b] for a, b in stairs]
                                   + [jnp.full((n_cand - len(stairs), n), BELOW_NEG_INF, F32)], axis=0)
            chosen = jnp.zeros(cand.shape, F32)
            work = cand
            for _ in range(PEER_TOPK):
                m = jnp.max(work, axis=0, keepdims=True)
                idx = jnp.min(jnp.where(work == m, cand_row, n_cand), axis=0, keepdims=True)
                hit = cand_row == idx
                chosen = jnp.where(hit, 1.0, chosen)
                work = jnp.where(hit, BELOW_NEG_INF, work)
            top_sum = vals[0][0] + vals[1][0]
            z = jnp.sum(chosen * jnp.exp(jnp.where(chosen > 0.5, cand, top_sum) - top_sum), axis=0, keepdims=True)
            rank1, s1 = ranks[0]
            rank2, s2 = ranks[1]
            cnt = jnp.zeros(rank1.shape, F32)
            for a in range(PEER_TOPK):
                rows_a = [r for r, (ka, _) in enumerate(stairs) if ka == a]
                c_a = chosen[rows_a[0]:rows_a[0] + 1]
                for r in rows_a[1:]:
                    c_a = c_a + chosen[r:r + 1]
                cnt = jnp.where(rank1 == np.float32(a), c_a, cnt)
            a_ref[hd] = jnp.exp(s1 - vals[0][0]) / z
            cnt_ref[hd] = cnt
            b_ref[hd] = jnp.exp(s2 - vals[1][0])
            r2_ref[hd] = rank2

    act = _gelu(_dot_nt(u_ref[...], h_ref[...]))
    wgt = jnp.zeros(act.shape, F32)
    for hd in range(PEER_HEADS):
        a_row = a_ref[hd, pl.ds(i, 1), :]
        c_row = cnt_ref[hd, pl.ds(i, 1), :]
        wgt = wgt + jnp.where(r2_ref[hd] < c_row, b_ref[hd] * a_row, 0.0)
    acc_ref[...] += _dot(vT_ref[...], (wgt * act).astype(BF16))

    @pl.when(i == pl.num_programs(1) - 1)
    def _():
        z = acc_ref[...].T
        zn = z * lax.rsqrt(jnp.mean(z * z, axis=-1, keepdims=True) + NORM_EPS) * gp_ref[...]
        o_ref[...] = x_ref[...] + gt_ref[...] * zn


def _peer_layer(x, g_pre, mod, tokens_per_row, g_post, w_q, sub_keys, u_bf, vT_bf, tm):
    n = x.shape[0]
    wqT = w_q.astype(BF16).T
    keys = sub_keys.reshape(PEER_HEADS * 2, PEER_KEYS, PEER_HALF)
    n_chunks = u_bf.shape[0] // PEER_KEYS
    row1 = lambda a: a.reshape(1, -1)
    g2, gp = row1(g_pre), row1(g_post)
    full = lambda a: pl.BlockSpec(a.shape, lambda t, i: (0,) * a.ndim)
    modspec = lambda col: _mod_specs(mod, tm, tokens_per_row, col)
    xspec = pl.BlockSpec((tm, D_MODEL), lambda t, i: (t, 0))
    per_head = pltpu.VMEM((PEER_HEADS, PEER_KEYS, tm), F32)
    return pl.pallas_call(
        _peer_kernel,
        grid=(n // tm, n_chunks),
        in_specs=[xspec, full(g2), modspec(4), modspec(3), modspec(5), full(gp), full(wqT), full(keys),
                  pl.BlockSpec((PEER_KEYS, D_MODEL), lambda t, i: (i, 0)),
                  pl.BlockSpec((D_MODEL, PEER_KEYS), lambda t, i: (0, i))],
        out_specs=xspec,
        out_shape=jax.ShapeDtypeStruct((n, D_MODEL), F32),
        scratch_shapes=[pltpu.VMEM((tm, D_MODEL), BF16), per_head, per_head, per_head, per_head,
                        pltpu.VMEM((D_MODEL, tm), F32)],
        compiler_params=_cparams(("arbitrary", "arbitrary")),
        name="peer",
    )(x, g2, mod, mod, mod, gp, wqT, keys, u_bf, vT_bf)


def kernel(x_prompt, x_sample, c_prompt, c_sample, cache_nsa, page_table, cache_win, state_wkv, state_shift, w_ada, b_ada, norm_pre1, norm_post1, norm_pre2, norm_post2, w_in, w_out, cmp_k_w1, cmp_k_w2, cmp_k_pe, cmp_v_w1, cmp_v_w2, cmp_v_pe, rwkv_mu, rwkv_w0, rwkv_w2, rwkv_a0, rwkv_a2, rwkv_g2, rwkv_k_k, rwkv_k_a, rwkv_r_k, rwkv_ln_w, rwkv_ln_b, peer_w_q, peer_sub_keys, peer_u, peer_v):
    batch, seq, d = x_prompt.shape
    dec_batch, dec_seq, _ = x_sample.shape
    depth = w_ada.shape[0]
    assert d == D_MODEL and seq % WINDOW == 0 and seq & (seq - 1) == 0 and dec_seq & (dec_seq - 1) == 0
    assert batch % SCAN_SEQS == 0 and dec_batch % SCAN_SEQS == 0
    n_p, n_s = batch * seq, dec_batch * dec_seq
    tile = lambda n: min(256, n)
    xp = x_prompt.reshape(n_p, d)
    xs = x_sample.reshape(n_s, d)
    c_all = jnp.concatenate([c_prompt, c_sample], axis=0)
    c_all = jnp.pad(c_all, ((0, -c_all.shape[0] % SUBLANES), (0, 0)))
    w_keep = min(WINDOW, seq)
    outs = [[] for _ in range(8)]
    for l in range(depth):
        mod = _modulation(c_all, w_ada[l], b_ada[l])
        mod_p = mod[:batch].reshape(batch, 1, 6 * d)
        mod_s = jnp.repeat(mod[batch:batch + dec_batch], dec_seq, axis=0)
        kw = _compress_weights(cmp_k_w1[l], cmp_k_w2[l], cmp_k_pe[l])
        vw = _compress_weights(cmp_v_w1[l], cmp_v_w2[l], cmp_v_pe[l])
        rwkv_w = (rwkv_mu[l], rwkv_w0[l], rwkv_w2[l], rwkv_a0[l], rwkv_a2[l], rwkv_g2[l],
                  rwkv_k_k[l], rwkv_k_a[l], rwkv_r_k[l], rwkv_ln_w[l], rwkv_ln_b[l])
        u_bf = peer_u[l].astype(BF16)
        vT_bf = peer_v[l].astype(BF16).T

        qT, kv, kvb, kvT, gT, rw = _in_projection(xp, norm_pre1[l], mod_p, seq, w_in[l], tile(n_p))
        kc, vcT, kbias, vbias = _compress_prompt(kv, batch, seq, kw, vw)
        nsa = _nsa_prompt(qT, kvb, kvT, gT, kc, vcT, batch, seq)
        r, w, k, kk, b, v, g, bonus = _rwkv_pre(rw, jnp.zeros((batch, 1, RWKV_PROJ), F32), seq, tile(n_p), rwkv_w)
        y, wkv_p = _rwkv_scan(r, w, k, kk, b, v, jnp.zeros((batch, RWKV_HEADS, HEAD_DIM, HEAD_DIM), F32),
                              batch, seq, min(256, seq))
        x1 = _mixer_out(xp, nsa, y, bonus, g, mod_p, seq, rwkv_ln_w[l], rwkv_ln_b[l], w_out[l], norm_post1[l], tile(n_p))
        xp = _peer_layer(x1, norm_pre2[l], mod_p, seq, norm_post2[l], peer_w_q[l], peer_sub_keys[l], u_bf, vT_bf, tile(n_p))
        kv3 = kv.reshape(batch, seq, KV6)
        outs[0].append(kv3[:, :, :4 * KV_WIDTH].reshape(batch, seq, 4, NSA_KV_HEADS, HEAD_DIM))
        outs[2].append(kv3[:, seq - w_keep:, 4 * KV_WIDTH:].reshape(batch, w_keep, 2, NSA_KV_HEADS, HEAD_DIM))
        outs[4].append(wkv_p)
        outs[6].append(rw.reshape(batch, seq, RWKV_PROJ)[:, -1])

        qT, kv, _, _, gT, rw = _in_projection(xs, norm_pre1[l], mod_s, None, w_in[l], tile(n_s))
        cache = cache_nsa[l].reshape(cache_nsa.shape[1], cache_nsa.shape[2], CACHE_PARTS * KV_WIDTH)
        nsa = _nsa_sample(qT.T, gT, kv, cache, page_table, cache_win[l], kw, vw, kbias, vbias, dec_batch, dec_seq)
        shift_tok = jnp.repeat(state_shift[l], dec_seq, axis=0)
        r, w, k, kk, b, v, g, bonus = _rwkv_pre(rw, shift_tok, dec_seq, tile(n_s), rwkv_w)
        y, wkv_s = _rwkv_scan(r, w, k, kk, b, v, state_wkv[l], dec_batch, dec_seq, dec_seq)
        x1 = _mixer_out(xs, nsa, y, bonus, g, mod_s, None, rwkv_ln_w[l], rwkv_ln_b[l], w_out[l], norm_post1[l], tile(n_s))
        xs = _peer_layer(x1, norm_pre2[l], mod_s, None, norm_post2[l], peer_w_q[l], peer_sub_keys[l], u_bf, vT_bf, tile(n_s))
        kv3 = kv.reshape(dec_batch, dec_seq, KV6)
        outs[1].append(kv3[:, :, :4 * KV_WIDTH].reshape(dec_batch, dec_seq, 4, NSA_KV_HEADS, HEAD_DIM))
        new_win = kv3[:, :, 4 * KV_WIDTH:].reshape(dec_batch, dec_seq, 2, NSA_KV_HEADS, HEAD_DIM)
        outs[3].append(jnp.concatenate([cache_win[l], new_win], axis=1)[:, dec_seq:])
        outs[5].append(wkv_s)
        outs[7].append(rw.reshape(dec_batch, dec_seq, RWKV_PROJ)[:, -1])
    return (xp.reshape(batch, seq, d), xs.reshape(dec_batch, dec_seq, d)) + tuple(jnp.stack(o) for o in outs)
```

```python
import functools

import numpy as np
import jax
import jax.numpy as jnp
from jax import lax
from jax.experimental import pallas as pl
from jax.experimental.pallas import tpu as pltpu

F32 = jnp.float32
BF16 = jnp.bfloat16
HIGHEST = lax.Precision.HIGHEST

LANES = 128
SUBLANES = 8
VMEM_LIMIT_BYTES = 56 * 1024 * 1024

D_MODEL = 1024
HEAD_DIM = 64
NORM_EPS = 1e-6
NEG_INF = -1e30
BELOW_NEG_INF = -3e38
NSA_HEADS = 8
NSA_KV_HEADS = 2
NSA_GROUP = NSA_HEADS // NSA_KV_HEADS
NSA_WIDTH = NSA_HEADS * HEAD_DIM
KV_WIDTH = NSA_KV_HEADS * HEAD_DIM
CMP_STRIDE = 16
CMP_LEN = 32
CMP_HIDDEN = 2 * HEAD_DIM
SEL_BLOCK = 64
SEL_TOPK = 16
WINDOW = 512
Q_BLOCK = 128
FORCE_BONUS = 1e4
ATTN_SCALE = HEAD_DIM ** -0.5
RWKV_HEADS = 8
RWKV_WIDTH = RWKV_HEADS * HEAD_DIM
DECAY_LORA = 64
ICLR_LORA = 64
GATE_LORA = 128
GN_EPS = 64e-5
NSA_PROJ = NSA_WIDTH + 6 * KV_WIDTH + 3 * NSA_HEADS
RWKV_PROJ = 3 * RWKV_WIDTH + DECAY_LORA + ICLR_LORA + GATE_LORA
KV6 = 6 * KV_WIDTH
GATE_ROWS = 32
PEER_HEADS = 8
PEER_KEYS = 128
PEER_HALF = 128
PEER_TOPK = 16
KEY_CHUNK = 128
SEL_CHUNK = 256


def _cparams(sem):
    return pltpu.CompilerParams(dimension_semantics=sem, vmem_limit_bytes=VMEM_LIMIT_BYTES)


def _gelu(x):
    return 0.5 * x * (1.0 + lax.erf(x * np.float32(0.7071067811865476)))


def _dot(a, b):
    return jnp.dot(a, b, preferred_element_type=F32)


def _dot_nt(a, b):
    return lax.dot_general(a, b, (((1,), (1,)), ((), ())), preferred_element_type=F32)


def _dot_hi(a, b):
    return jnp.dot(a, b, preferred_element_type=F32, precision=HIGHEST)


def _mod_kernel(c_ref, w_ref, b_ref, o_ref):
    c = c_ref[...]
    o_ref[...] = _dot_hi(c * jax.nn.sigmoid(c), w_ref[...]) + b_ref[...]


def _modulation(c, w_ada, b_ada):
    rows, d = c.shape
    n = w_ada.shape[1]
    tn = 1536
    return pl.pallas_call(
        _mod_kernel,
        grid=(n // tn,),
        in_specs=[pl.BlockSpec((rows, d), lambda j: (0, 0)),
                  pl.BlockSpec((d, tn), lambda j: (0, j)),
                  pl.BlockSpec((1, tn), lambda j: (0, j))],
        out_specs=pl.BlockSpec((rows, tn), lambda j: (0, j)),
        out_shape=jax.ShapeDtypeStruct((rows, n), F32),
        compiler_params=_cparams(("arbitrary",)),
        name="modulation",
    )(c, w_ada, b_ada.reshape(1, n))


def _inproj_kernel(x_ref, g_ref, sc_ref, sh_ref, wqT_ref, wkv_ref, wkvT_ref, wgT_ref, wr_ref,
                   qT_ref, kv_ref, kvb_ref, kvT_ref, gT_ref, rw_ref):
    x = x_ref[...]
    h = x * lax.rsqrt(jnp.mean(x * x, axis=-1, keepdims=True) + NORM_EPS) * g_ref[...]
    h = (h * (1.0 + sc_ref[...]) + sh_ref[...]).astype(BF16)
    qT_ref[...] = _dot_nt(wqT_ref[...], h)
    kv = _dot(h, wkv_ref[...])
    kv_ref[...] = kv
    kvb_ref[...] = kv.astype(BF16)
    kvT_ref[...] = _dot_nt(wkvT_ref[...], h).astype(BF16)
    gT_ref[...] = jax.nn.sigmoid(_dot_nt(wgT_ref[...], h))
    rw_ref[...] = _dot(h, wr_ref[...])


def _mod_specs(mod, tm, tokens_per_row, col):
    if tokens_per_row is None:
        return pl.BlockSpec((tm, D_MODEL), lambda i, *_: (i, col))
    tiles_per_row = tokens_per_row // tm
    return pl.BlockSpec((None, 1, D_MODEL), lambda i, *_: (i // tiles_per_row, 0, col))


def _in_projection(x, g_pre, mod, tokens_per_row, w_in, tm):
    n = x.shape[0]
    wb = w_in.astype(BF16)
    wq = wb[:, :NSA_WIDTH]
    wkv = wb[:, NSA_WIDTH:NSA_WIDTH + KV6]
    wg = jnp.pad(wb[:, NSA_WIDTH + KV6:NSA_PROJ], ((0, 0), (0, GATE_ROWS - 3 * NSA_HEADS)))
    wr = wb[:, NSA_PROJ:]
    full = lambda a: pl.BlockSpec(a.shape, lambda i: (0,) * a.ndim)
    wqT, wkvT, wgT = wq.T, wkv.T, wg.T
    g2 = g_pre.reshape(1, D_MODEL)
    outs = (
        jax.ShapeDtypeStruct((NSA_WIDTH, n), F32),
        jax.ShapeDtypeStruct((n, KV6), F32),
        jax.ShapeDtypeStruct((n, KV6), BF16),
        jax.ShapeDtypeStruct((KV6, n), BF16),
        jax.ShapeDtypeStruct((GATE_ROWS, n), F32),
        jax.ShapeDtypeStruct((n, RWKV_PROJ), F32),
    )
    return pl.pallas_call(
        _inproj_kernel,
        grid=(n // tm,),
        in_specs=[pl.BlockSpec((tm, D_MODEL), lambda i: (i, 0)), full(g2),
                  _mod_specs(mod, tm, tokens_per_row, 1), _mod_specs(mod, tm, tokens_per_row, 0),
                  full(wqT), full(wkv), full(wkvT), full(wgT), full(wr)],
        out_specs=(pl.BlockSpec((NSA_WIDTH, tm), lambda i: (0, i)),
                   pl.BlockSpec((tm, KV6), lambda i: (i, 0)),
                   pl.BlockSpec((tm, KV6), lambda i: (i, 0)),
                   pl.BlockSpec((KV6, tm), lambda i: (0, i)),
                   pl.BlockSpec((GATE_ROWS, tm), lambda i: (0, i)),
                   pl.BlockSpec((tm, RWKV_PROJ), lambda i: (i, 0))),
        out_shape=outs,
        compiler_params=_cparams(("arbitrary",)),
        name="in_projection",
    )(x, g2, mod, mod, wqT, wkv, wkvT, wgT, wr)


def _compress_weights(w1, w2, pe):
    z = jnp.zeros((CMP_STRIDE, HEAD_DIM, CMP_HIDDEN), F32)
    a, b = w1[:CMP_STRIDE], w1[CMP_STRIDE:]
    top = jnp.concatenate([a, z, b, z], axis=-1)
    bot = jnp.concatenate([z, a, z, b], axis=-1)
    wbd = jnp.concatenate([top, bot], axis=1).astype(BF16)
    z2 = jnp.zeros((CMP_HIDDEN, HEAD_DIM), F32)
    w2bd = jnp.concatenate([jnp.concatenate([w2, z2], axis=1),
                            jnp.concatenate([z2, w2], axis=1)], axis=0).astype(BF16)
    return wbd, w2bd, w2bd.T, pe.reshape(1, CMP_LEN * HEAD_DIM), w1.reshape(CMP_LEN * HEAD_DIM, CMP_HIDDEN)


def _compress_bias(pe_ref, w1_ref):
    b = _dot_hi(jnp.broadcast_to(pe_ref[...], (SUBLANES, CMP_LEN * HEAD_DIM)), w1_ref[...])[0:1]
    return jnp.concatenate([b, b], axis=1)


def _compress_hidden(rows_at, n_chunks, wbd_ref, bias):
    acc = jnp.zeros((n_chunks, 4 * CMP_HIDDEN), F32)
    for l in range(CMP_STRIDE):
        acc = acc + _dot(rows_at(l).astype(BF16), wbd_ref[l])
    first, second = acc[:, :2 * CMP_HIDDEN], acc[:, 2 * CMP_HIDDEN:]
    hid = first + pltpu.roll(second, n_chunks - 1, 0) + bias
    return _gelu(hid).astype(BF16)


def _cmp_prompt_kernel(k_ref, v_ref, kwbd_ref, kw2_ref, kpe_ref, kw1_ref, vwbd_ref, vw2T_ref, vpe_ref, vw1_ref,
                       kc_ref, vcT_ref, kb_ref, vb_ref):
    n_chunks = k_ref.shape[0] // CMP_STRIDE
    kb = _compress_bias(kpe_ref, kw1_ref)
    vb = _compress_bias(vpe_ref, vw1_ref)
    kb_ref[...] = kb
    vb_ref[...] = vb
    gk = _compress_hidden(lambda l: k_ref[pl.ds(l, n_chunks, stride=CMP_STRIDE), :], n_chunks, kwbd_ref, kb)
    kc_ref[...] = _dot(gk, kw2_ref[...]).astype(BF16)
    gv = _compress_hidden(lambda l: v_ref[pl.ds(l, n_chunks, stride=CMP_STRIDE), :], n_chunks, vwbd_ref, vb)
    vcT_ref[...] = _dot_nt(vw2T_ref[...], gv).astype(BF16)


def _compress_prompt(kv, batch, seq, kw, vw):
    nc = seq // CMP_STRIDE
    kwbd, kw2, _, kpe, kw1 = kw
    vwbd, _, vw2T, vpe, vw1 = vw
    full = lambda a: pl.BlockSpec(a.shape, lambda b: (0,) * a.ndim)
    return pl.pallas_call(
        _cmp_prompt_kernel,
        grid=(batch,),
        in_specs=[pl.BlockSpec((seq, KV_WIDTH), lambda b: (b, 0)),
                  pl.BlockSpec((seq, KV_WIDTH), lambda b: (b, 1)),
                  full(kwbd), full(kw2), full(kpe), full(kw1), full(vwbd), full(vw2T), full(vpe), full(vw1)],
        out_specs=(pl.BlockSpec((None, nc, KV_WIDTH), lambda b: (b, 0, 0)),
                   pl.BlockSpec((None, KV_WIDTH, nc), lambda b: (b, 0, 0)),
                   pl.BlockSpec((1, 2 * CMP_HIDDEN), lambda b: (0, 0)),
                   pl.BlockSpec((1, 2 * CMP_HIDDEN), lambda b: (0, 0))),
        out_shape=(jax.ShapeDtypeStruct((batch, nc, KV_WIDTH), BF16),
                   jax.ShapeDtypeStruct((batch, KV_WIDTH, nc), BF16),
                   jax.ShapeDtypeStruct((1, 2 * CMP_HIDDEN), F32),
                   jax.ShapeDtypeStruct((1, 2 * CMP_HIDDEN), F32)),
        compiler_params=_cparams(("arbitrary",)),
        name="compress_prompt",
    )(kv, kv, kwbd, kw2, kpe, kw1, vwbd, vw2T, vpe, vw1)


def _overlap_T(n_sel_rows, n_cmp_cols, n_sel, n_cmp):
    s = np.arange(n_sel_rows)[:, None] * SEL_BLOCK
    c = np.arange(n_cmp_cols)[None, :] * CMP_STRIDE
    ov = (c < s + SEL_BLOCK) & (c + CMP_LEN > s)
    ov &= (np.arange(n_sel_rows)[:, None] < n_sel) & (np.arange(n_cmp_cols)[None, :] < n_cmp)
    return jnp.asarray(ov.astype(np.float32))


def _masked_softmax_cols(s, mask):
    s = jnp.where(mask, s, NEG_INF)
    m = jnp.max(s, axis=0, keepdims=True)
    e = jnp.where(mask, jnp.exp(s - m), 0.0)
    return e / jnp.maximum(jnp.sum(e, axis=0, keepdims=True), 1e-30)


def _select_blocks(imp, q_pos, n_sel):
    rows = imp.shape[0]
    blk = lax.broadcasted_iota(jnp.int32, imp.shape, 0)
    cur = jnp.right_shift(q_pos, 6)
    valid = blk <= cur
    forced = (blk == 0) | (blk == cur) | (blk == cur - 1)
    score = jnp.where(valid, imp + jnp.where(forced, FORCE_BONUS, 0.0), NEG_INF)
    score = jnp.where(blk < n_sel, score, BELOW_NEG_INF)
    sel = jnp.zeros(imp.shape, F32)
    for _ in range(min(SEL_TOPK, n_sel)):
        m = jnp.max(score, axis=0, keepdims=True)
        idx = jnp.min(jnp.where(score == m, blk, rows), axis=0, keepdims=True)
        hit = blk == idx
        sel = jnp.where(hit & (m > 0.5 * NEG_INF), 1.0, sel)
        score = jnp.where(hit, BELOW_NEG_INF, score)
    return sel


def _online_update(carry, s, mask, weighted_values):
    m, l, acc = carry
    s = jnp.where(mask, s, NEG_INF)
    m_new = jnp.maximum(m, jnp.max(s, axis=0, keepdims=True))
    alpha = jnp.exp(m - m_new)
    e = jnp.where(mask, jnp.exp(s - m_new), 0.0)
    l = alpha * l + jnp.sum(e, axis=0, keepdims=True)
    acc = alpha * acc + weighted_values(e.astype(BF16))
    return m_new, l, acc


def _online_init(rows, cols):
    return (jnp.full((1, cols), NEG_INF, F32), jnp.zeros((1, cols), F32), jnp.zeros((rows, cols), F32))


def _online_finish(carry):
    _, l, acc = carry
    return acc / jnp.maximum(l, 1e-30)


def _nsa_prompt_kernel(qT_ref, ksel_ref, kwin_ref, vselT_ref, vwinT_ref, kc_ref, vcT_ref, gT_ref, ovT_ref,
                       o_ref, sel_ref):
    qb = pl.program_id(1)
    n_cmp_rows = kc_ref.shape[0]
    n_sel = ovT_ref.shape[0]
    cols = NSA_GROUP * Q_BLOCK
    pos0 = qb * Q_BLOCK
    qT = qT_ref[...] * ATTN_SCALE
    gates = gT_ref[...]
    q_pos1 = pos0 + lax.broadcasted_iota(jnp.int32, (1, Q_BLOCK), 1)
    q_pos = jnp.concatenate([q_pos1] * NSA_GROUP, axis=1)
    key_row = lax.broadcasted_iota(jnp.int32, (KEY_CHUNK, cols), 0)
    sel_row = lax.broadcasted_iota(jnp.int32, (SEL_CHUNK, cols), 0)
    zeros_half = jnp.zeros((HEAD_DIM, cols), F32)
    out_rows = []
    for g in range(NSA_KV_HEADS):
        qg = jnp.concatenate([qT[(g * NSA_GROUP + r) * HEAD_DIM:(g * NSA_GROUP + r + 1) * HEAD_DIM, :]
                              for r in range(NSA_GROUP)], axis=1)
        qbd = jnp.concatenate([qg, zeros_half] if g == 0 else [zeros_half, qg], axis=0).astype(BF16)
        lo, hi = g * HEAD_DIM, (g + 1) * HEAD_DIM

        s = _dot(kc_ref[...], qbd)
        n_idx = lax.broadcasted_iota(jnp.int32, s.shape, 0)
        cmask = (n_idx * CMP_STRIDE + (CMP_LEN - 1) <= q_pos) & (n_idx < n_cmp_rows - 1)
        p = _masked_softmax_cols(s, cmask)
        o_c = _dot(vcT_ref[lo:hi, :], p.astype(BF16))
        p_sum = p[:, 0:Q_BLOCK]
        for r in range(1, NSA_GROUP):
            p_sum = p_sum + p[:, r * Q_BLOCK:(r + 1) * Q_BLOCK]
        imp = _dot_hi(ovT_ref[...], p_sum)
        sel = _select_blocks(imp, q_pos1, n_sel)
        sel_ref[...] = jnp.concatenate([sel] * NSA_GROUP, axis=1)

        def sel_step(j, carry):
            keys = pl.ds(pl.multiple_of(j * SEL_CHUNK, SEL_CHUNK), SEL_CHUNK)
            sc = _dot(ksel_ref[keys, :], qbd)
            blocks_per_chunk = SEL_CHUNK // SEL_BLOCK
            chosen = jnp.concatenate(
                [jnp.broadcast_to(sel_ref[pl.ds(j * blocks_per_chunk + i, 1), :], (SEL_BLOCK, cols))
                 for i in range(blocks_per_chunk)], axis=0) > 0.5
            mask = chosen & (j * SEL_CHUNK + sel_row <= q_pos)
            return _online_update(carry, sc, mask, functools.partial(_dot, vselT_ref[lo:hi, keys]))
        n_steps = (pos0 + Q_BLOCK + SEL_CHUNK - 1) // SEL_CHUNK
        o_s = _online_finish(lax.fori_loop(0, n_steps, sel_step, _online_init(HEAD_DIM, cols)))

        def win_step(i, carry):
            jj = qb - WINDOW // KEY_CHUNK + i
            jc = jnp.maximum(jj, 0)
            k = kwin_ref[pl.ds(pl.multiple_of(jc * KEY_CHUNK, KEY_CHUNK), KEY_CHUNK), :]
            sc = _dot(k, qbd)
            k_pos = jj * KEY_CHUNK + key_row
            mask = (k_pos <= q_pos) & (q_pos - k_pos < WINDOW) & (k_pos >= 0)
            vT = vwinT_ref[lo:hi, pl.ds(pl.multiple_of(jc * KEY_CHUNK, KEY_CHUNK), KEY_CHUNK)]
            return _online_update(carry, sc, mask, functools.partial(_dot, vT))
        o_w = _online_finish(lax.fori_loop(0, WINDOW // KEY_CHUNK + 1, win_step, _online_init(HEAD_DIM, cols)))

        for r in range(NSA_GROUP):
            h = g * NSA_GROUP + r
            cs = slice(r * Q_BLOCK, (r + 1) * Q_BLOCK)
            out_rows.append(gates[3 * h:3 * h + 1] * o_c[:, cs] + gates[3 * h + 1:3 * h + 2] * o_s[:, cs]
                            + gates[3 * h + 2:3 * h + 3] * o_w[:, cs])
    o_ref[...] = jnp.concatenate(out_rows, axis=0).T


def _nsa_prompt(qT, kvb, kvT, gT, kc, vcT, batch, seq):
    nq = seq // Q_BLOCK
    nc = seq // CMP_STRIDE
    n_sel = seq // SEL_BLOCK
    ovT = _overlap_T(n_sel, nc, n_sel, nc - 1)
    cols = NSA_GROUP * Q_BLOCK
    return pl.pallas_call(
        _nsa_prompt_kernel,
        grid=(batch, nq),
        in_specs=[pl.BlockSpec((NSA_WIDTH, Q_BLOCK), lambda b, q: (0, b * nq + q)),
                  pl.BlockSpec((seq, KV_WIDTH), lambda b, q: (b, 2)),
                  pl.BlockSpec((seq, KV_WIDTH), lambda b, q: (b, 4)),
                  pl.BlockSpec((KV_WIDTH, seq), lambda b, q: (3, b)),
                  pl.BlockSpec((KV_WIDTH, seq), lambda b, q: (5, b)),
                  pl.BlockSpec((None, nc, KV_WIDTH), lambda b, q: (b, 0, 0)),
                  pl.BlockSpec((None, KV_WIDTH, nc), lambda b, q: (b, 0, 0)),
                  pl.BlockSpec((GATE_ROWS, Q_BLOCK), lambda b, q: (0, b * nq + q)),
                  pl.BlockSpec((n_sel, nc), lambda b, q: (0, 0))],
        out_specs=pl.BlockSpec((Q_BLOCK, NSA_WIDTH), lambda b, q: (b * nq + q, 0)),
        out_shape=jax.ShapeDtypeStruct((batch * seq, NSA_WIDTH), F32),
        scratch_shapes=[pltpu.VMEM((n_sel, cols), F32)],
        compiler_params=_cparams(("arbitrary", "arbitrary")),
        name="nsa_prompt",
    )(qT, kvb, kvb, kvT, kvT, kc, vcT, gT, ovT)


CACHE_PARTS = 4
CACHE_CHANNELS = CACHE_PARTS * KV_WIDTH
TRANSPOSE_CHUNK = 512


def _softmax_two(s_a, mask_a, s_b, mask_b):
    m = jnp.maximum(jnp.max(jnp.where(mask_a, s_a, NEG_INF), axis=-1, keepdims=True),
                    jnp.max(jnp.where(mask_b, s_b, NEG_INF), axis=-1, keepdims=True))
    e_a = jnp.where(mask_a, jnp.exp(s_a - m), 0.0)
    e_b = jnp.where(mask_b, jnp.exp(s_b - m), 0.0)
    total = jnp.sum(e_a, axis=-1, keepdims=True) + jnp.sum(e_b, axis=-1, keepdims=True)
    return e_a.astype(BF16), e_b.astype(BF16), 1.0 / jnp.maximum(total, 1e-30)


def _nsa_sample_kernel(dec_seq, n_sel, pt_ref, q_ref, gate_ref, newT_ref, winT_ref, cache_ref, expand_hbm,
                       kwbd_ref, kw2_ref, kb_ref, vwbd_ref, vw2_ref, vb_ref, ovT_ref, fold_ref,
                       o_ref, buf_ref, rows_ref, expand_ref, sem_ref, esem_ref):
    b = pl.program_id(0)
    nb = pl.num_programs(0)
    n_pages = pt_ref.shape[1]
    page = cache_ref.shape[2]
    p_len = n_pages * page
    n_chunks = p_len // CMP_STRIDE
    n_rows = o_ref.shape[0]
    slot = b % 2

    def page_copy(seq, i, sl):
        return pltpu.make_async_copy(cache_ref.at[pt_ref[seq, i]],
                                     buf_ref.at[sl, :, pl.ds(i * page, page)], sem_ref.at[sl])

    def start_fetch(seq, sl):
        def body(i, _):
            page_copy(seq, i, sl).start()
            return 0
        lax.fori_loop(0, n_pages, body, 0)

    expand_copy = pltpu.make_async_copy(expand_hbm, expand_ref, esem_ref.at[0])

    @pl.when(b == 0)
    def _():
        expand_copy.start()
        start_fetch(0, 0)

    @pl.when(b + 1 < nb)
    def _():
        start_fetch(b + 1, 1 - slot)

    def wait_body(i, _):
        page_copy(b, i, slot).wait()
        return 0
    lax.fori_loop(0, n_pages, wait_body, 0)

    @pl.when(b == 0)
    def _():
        expand_copy.wait()

    q_all = (q_ref[...] * ATTN_SCALE).astype(BF16)
    col = lax.broadcasted_iota(jnp.int32, (1, LANES), 1)
    q_pos = p_len + (col & (dec_seq - 1))

    def compressed(part, wbd_ref, bias_ref):
        def move(j, _):
            cols = pl.ds(pl.multiple_of(j * TRANSPOSE_CHUNK, TRANSPOSE_CHUNK), TRANSPOSE_CHUNK)
            rows_ref[cols, :] = buf_ref[slot, part * KV_WIDTH:(part + 1) * KV_WIDTH, cols].T
            return 0
        lax.fori_loop(0, p_len // TRANSPOSE_CHUNK, move, 0)
        return _compress_hidden(lambda l: rows_ref[pl.ds(l, n_chunks, stride=CMP_STRIDE), :], n_chunks,
                                wbd_ref, bias_ref[...])

    kc = _dot(compressed(0, kwbd_ref, kb_ref), kw2_ref[...]).astype(BF16)
    vc = _dot(compressed(1, vwbd_ref, vb_ref), vw2_ref[...]).astype(BF16)

    s = _dot_nt(kc, q_all)
    n_idx = lax.broadcasted_iota(jnp.int32, s.shape, 0)
    cmask = (n_idx * CMP_STRIDE + (CMP_LEN - 1) <= q_pos) & (n_idx < n_chunks - 1)
    p = _masked_softmax_cols(s, cmask)
    o_c = lax.dot_general(p.astype(BF16), vc, (((0,), (0,)), ((), ())), preferred_element_type=F32)
    imp = _dot_hi(_dot_hi(ovT_ref[...], p), fold_ref[...])
    sel = _select_blocks(imp, q_pos, n_sel)
    selT = _dot_hi(sel, fold_ref[...].T).T[:n_rows]

    q_rows = q_all[:n_rows]
    row_tok = lax.broadcasted_iota(jnp.int32, (n_rows, 1), 0) & (dec_seq - 1)
    lane = lax.broadcasted_iota(jnp.int32, (n_rows, LANES), 1)
    new_ok = (lane <= row_tok) & (lane < dec_seq)
    newT = newT_ref[...].astype(BF16)
    part = lambda a, i: a[i * KV_WIDTH:(i + 1) * KV_WIDTH]

    chosen = _dot(selT.astype(BF16), expand_ref[...]) > 0.5
    s_past = _dot(q_rows, buf_ref[slot, 2 * KV_WIDTH:3 * KV_WIDTH, :].astype(BF16))
    new_blk = p_len // SEL_BLOCK
    new_sel = (selT[:, new_blk:new_blk + 1] > 0.5) & new_ok
    e_p, e_n, inv = _softmax_two(s_past, chosen, _dot(q_rows, part(newT, 2)), new_sel)
    o_s = (_dot_nt(e_p, buf_ref[slot, 3 * KV_WIDTH:4 * KV_WIDTH, :].astype(BF16)) + _dot_nt(e_n, part(newT, 3))) * inv

    winT = winT_ref[...].astype(BF16)
    w_buf = winT.shape[1]
    k_pos = p_len - w_buf + lax.broadcasted_iota(jnp.int32, (n_rows, w_buf), 1)
    wmask = (p_len + row_tok - k_pos < WINDOW) & (k_pos >= 0)
    e_w, e_n, inv = _softmax_two(_dot(q_rows, winT[:KV_WIDTH]), wmask, _dot(q_rows, part(newT, 4)), new_ok)
    o_w = (_dot_nt(e_w, winT[KV_WIDTH:]) + _dot_nt(e_n, part(newT, 5))) * inv

    o_ref[...] = gate_ref[0] * o_c[:n_rows] + gate_ref[1] * o_s + gate_ref[2] * o_w


def _nsa_sample(q, gT, kv_new, cacheT, page_table, winT, kw, vw, kbias, vbias, dec_batch, dec_seq):
    n_pages = page_table.shape[1]
    page = cacheT.shape[2]
    p_len = n_pages * page
    n_chunks = p_len // CMP_STRIDE
    n_sel = -(-(p_len + dec_seq) // SEL_BLOCK)
    sel_rows = -(-n_sel // LANES) * LANES
    n_rows = NSA_HEADS * dec_seq
    q5 = q.reshape(dec_batch, dec_seq, NSA_KV_HEADS, NSA_GROUP, HEAD_DIM).transpose(0, 2, 3, 1, 4)
    eye = jnp.eye(NSA_KV_HEADS, dtype=F32)
    qrows = (q5[:, :, :, :, None, :] * eye[None, :, None, None, :, None]).reshape(dec_batch, n_rows, KV_WIDTH)
    qrows = jnp.pad(qrows, ((0, 0), (0, LANES - n_rows), (0, 0)))
    g4 = gT[:3 * NSA_HEADS].reshape(NSA_HEADS, 3, dec_batch, dec_seq).transpose(2, 1, 0, 3).reshape(dec_batch, 3, n_rows)
    gexp = jnp.broadcast_to(g4[..., None], (dec_batch, 3, n_rows, LANES))
    newT = jnp.pad(kv_new.reshape(dec_batch, dec_seq, KV6).transpose(0, 2, 1), ((0, 0), (0, 0), (0, LANES - dec_seq)))
    ovT = _overlap_T(sel_rows, n_chunks, n_sel, n_chunks - 1)
    fold = np.zeros((LANES, LANES), np.float32)
    for g in range(NSA_KV_HEADS):
        for r in range(NSA_GROUP):
            for s in range(dec_seq):
                fold[(g * NSA_GROUP + r) * dec_seq + s, g * dec_seq + s] = 1.0
    fold = jnp.asarray(fold)
    expand = jnp.asarray(np.arange(sel_rows)[:, None] == (np.arange(p_len) // SEL_BLOCK)[None, :]).astype(BF16)
    kwbd, kw2, _, _, _ = kw
    vwbd, vw2, _, _, _ = vw
    full = lambda a: pl.BlockSpec(a.shape, lambda b, pt: (0,) * a.ndim)
    per_seq = lambda a: pl.BlockSpec((None,) + a.shape[1:], lambda b, pt: (b,) + (0,) * (a.ndim - 1))
    hbm = pl.BlockSpec(memory_space=pl.ANY)
    grid_spec = pltpu.PrefetchScalarGridSpec(
        num_scalar_prefetch=1,
        grid=(dec_batch,),
        in_specs=[per_seq(qrows), per_seq(gexp), per_seq(newT), per_seq(winT), hbm, hbm,
                  full(kwbd), full(kw2), full(kbias), full(vwbd), full(vw2), full(vbias), full(ovT), full(fold)],
        out_specs=pl.BlockSpec((None, n_rows, KV_WIDTH), lambda b, pt: (b, 0, 0)),
        scratch_shapes=[pltpu.VMEM((2, CACHE_CHANNELS, p_len), F32),
                        pltpu.VMEM((p_len, KV_WIDTH), F32),
                        pltpu.VMEM((sel_rows, p_len), BF16),
                        pltpu.SemaphoreType.DMA((2,)),
                        pltpu.SemaphoreType.DMA((1,))])
    o = pl.pallas_call(
        functools.partial(_nsa_sample_kernel, dec_seq, n_sel),
        grid_spec=grid_spec,
        out_shape=jax.ShapeDtypeStruct((dec_batch, n_rows, KV_WIDTH), F32),
        compiler_params=_cparams(("arbitrary",)),
        name="nsa_sample",
    )(page_table, qrows, gexp, newT, winT, cacheT, expand, kwbd, kw2, kbias, vwbd, vw2, vbias, ovT, fold)
    o = o.reshape(dec_batch, NSA_KV_HEADS, NSA_GROUP, dec_seq, NSA_KV_HEADS, HEAD_DIM)
    o = jnp.stack([o[:, g, :, :, g] for g in range(NSA_KV_HEADS)], axis=1)
    return o.transpose(0, 3, 1, 2, 4).reshape(dec_batch * dec_seq, NSA_WIDTH)


def _head_ones(width):
    h = np.arange(width) // HEAD_DIM
    return (h[:, None] == h[None, :]).astype(np.float32)


def _rwkv_pre_kernel(seq, use_carry, rw_ref, shift_ref, mu_ref, w0_ref, a0_ref, kk_ref, ka_ref, rk_ref,
                     w2_ref, a2_ref, g2_ref, ones_ref,
                     r_ref, w_ref, k_ref, kkn_ref, b_ref, v_ref, g_ref, bonus_ref, carry_ref):
    i = pl.program_id(0)
    x = rw_ref[...]
    tm = x.shape[0]
    row = lax.broadcasted_iota(jnp.int32, (tm, 1), 0)
    prev = pltpu.roll(x, 1, 0)
    if use_carry:
        @pl.when(i == 0)
        def _():
            carry_ref[...] = jnp.zeros(carry_ref.shape, F32)
        prev = jnp.where(row == 0, carry_ref[...], prev)
        carry_ref[...] = x[tm - 1:tm]
    prev = jnp.where(((i * tm + row) & (seq - 1)) == 0, shift_ref[...], prev)
    xm = x + (prev - x) * mu_ref[...]
    wd = RWKV_WIDTH
    r, k, v = xm[:, :wd], xm[:, wd:2 * wd], xm[:, 2 * wd:3 * wd]
    lwa = xm[:, 3 * wd:3 * wd + DECAY_LORA + ICLR_LORA]
    lg = xm[:, 3 * wd + DECAY_LORA + ICLR_LORA:]
    w = -jax.nn.softplus(-(w0_ref[...] + _dot_hi(jnp.tanh(lwa), w2_ref[...]))) - 0.5
    a = jax.nn.sigmoid(a0_ref[...] + _dot_hi(lwa, a2_ref[...]))
    ones = ones_ref[...]
    kk = k * kk_ref[...]
    kk = kk / jnp.maximum(jnp.sqrt(_dot_hi(kk * kk, ones)), 1e-12)
    k2 = k * (1.0 + (a - 1.0) * ka_ref[...])
    r_ref[...] = r
    w_ref[...] = jnp.exp(-jnp.exp(w))
    k_ref[...] = k2
    kkn_ref[...] = kk
    b_ref[...] = kk * a
    v_ref[...] = v
    g_ref[...] = _dot_hi(jax.nn.sigmoid(lg), g2_ref[...])
    bonus_ref[...] = _dot_hi(r * k2 * rk_ref[...], ones) * v


def _rwkv_pre(rw, shift, seq, tm, rwkv_w):
    mu, w0, w2, a0, a2, g2, k_k, k_a, r_k, _, _ = rwkv_w
    n = rw.shape[0]
    use_carry = seq >= tm
    row1 = lambda a: a.reshape(1, -1)
    z = jnp.zeros((DECAY_LORA, RWKV_WIDTH), F32)
    w2p = jnp.concatenate([w2, z], axis=0)
    a2p = jnp.concatenate([z, a2], axis=0)
    ones = jnp.asarray(_head_ones(RWKV_WIDTH))
    consts = [row1(mu), row1(w0), row1(a0), row1(k_k), row1(k_a), row1(r_k), w2p, a2p, g2, ones]
    full = lambda a: pl.BlockSpec(a.shape, lambda i: (0,) * a.ndim)
    if use_carry:
        shift_spec = pl.BlockSpec((None, 1, RWKV_PROJ), lambda i: (i // (seq // tm), 0, 0))
    else:
        shift_spec = pl.BlockSpec((tm, RWKV_PROJ), lambda i: (i, 0))
    tile = pl.BlockSpec((tm, RWKV_WIDTH), lambda i: (i, 0))
    return pl.pallas_call(
        functools.partial(_rwkv_pre_kernel, seq, use_carry),
        grid=(n // tm,),
        in_specs=[pl.BlockSpec((tm, RWKV_PROJ), lambda i: (i, 0)), shift_spec] + [full(c) for c in consts],
        out_specs=(tile,) * 8,
        out_shape=(jax.ShapeDtypeStruct((n, RWKV_WIDTH), F32),) * 8,
        scratch_shapes=[pltpu.VMEM((1, RWKV_PROJ), F32)],
        compiler_params=_cparams(("arbitrary",)),
        name="rwkv_pre",
    )(rw, shift, *consts)


SCAN_SEQS = 2
SCAN_UNROLL = 2


def _split3(x):
    hi = x.astype(BF16)
    r1 = x - hi.astype(F32)
    mid = r1.astype(BF16)
    lo = (r1 - mid.astype(F32)).astype(BF16)
    return hi, mid, lo


def _dot3_stacked(pieces, w):
    rows = pieces[0].shape[0]
    parts = [part for x in pieces for part in _split3(x)]
    out = _dot(jnp.concatenate(parts, axis=0), w)
    return [out[(3 * i) * rows:(3 * i + 1) * rows] + out[(3 * i + 1) * rows:(3 * i + 2) * rows]
            + out[(3 * i + 2) * rows:(3 * i + 3) * rows] for i in range(len(pieces))]


def _head_sums(ps, ones_half):
    half = ps[0].shape[1] // 2
    out = _dot3_stacked([p[:, :half] for p in ps] + [p[:, half:] for p in ps], ones_half)
    return [jnp.concatenate([out[i], out[len(ps) + i]], axis=1) for i in range(len(ps))]


def _scan_kernel(tc, r_ref, w_ref, k_ref, kk_ref, b_ref, vts_ref, s0_ref, ones_ref, wv_ref, wy_ref,
                 y_ref, s_ref, ypart_ref):
    tb = pl.program_id(1)
    n_blocks = vts_ref.shape[1]

    @pl.when(tb == 0)
    def _():
        s_ref[...] = s0_ref[...]

    ones_half = ones_ref[...]
    rows = (HEAD_DIM, RWKV_WIDTH)

    def block(blk, _):
        vts = [vts_ref[s, blk] for s in range(SCAN_SEQS)]

        def step(tp, _):
            t = blk * tc + tp
            bc = lambda ref, s: jnp.broadcast_to(ref[s, pl.ds(t, 1), :], rows)
            seqs = range(SCAN_SEQS)
            st = [s_ref[s] for s in seqs]
            sa = _head_sums([st[s] * bc(kk_ref, s) for s in seqs], ones_half)
            vcol = _dot3_stacked(vts, wv_ref[tp])
            st = [st[s] * bc(w_ref, s) - sa[s] * bc(b_ref, s) + vcol[s] * bc(k_ref, s) for s in seqs]
            y = _dot3_stacked([st[s] * bc(r_ref, s) for s in seqs], wy_ref[tp])
            for s in seqs:
                s_ref[s] = st[s]
                ypart_ref[s, tp] = y[s]
            return 0

        lax.fori_loop(0, tc, step, 0, unroll=SCAN_UNROLL)
        for s in range(SCAN_SEQS):
            acc = ypart_ref[s, 0]
            for tp in range(1, tc):
                acc = acc + ypart_ref[s, tp]
            y_ref[s, blk] = acc
        return 0

    lax.fori_loop(0, n_blocks, block, 0)


def _scan_consts(tc):
    wv = np.zeros((tc, LANES, RWKV_WIDTH), np.float32)
    for tp in range(tc):
        for h in range(RWKV_HEADS):
            wv[tp, tp * RWKV_HEADS + h, h * HEAD_DIM:(h + 1) * HEAD_DIM] = 1.0
    wy = np.transpose(wv, (0, 2, 1))
    return (jnp.asarray(_head_ones(RWKV_WIDTH // 2)).astype(BF16), jnp.asarray(wv).astype(BF16),
            jnp.asarray(wy).astype(BF16))


def _rwkv_scan(r, w, k, kk, b, v, s0, batch, seq, tt):
    tc = min(LANES // RWKV_HEADS, seq)
    nb = seq // tc
    to3 = lambda a: a.reshape(batch, seq, RWKV_WIDTH)
    vts = v.reshape(batch, nb, tc, RWKV_HEADS, HEAD_DIM).transpose(0, 1, 4, 2, 3).reshape(batch, nb, HEAD_DIM, tc * RWKV_HEADS)
    vts = jnp.pad(vts, ((0, 0), (0, 0), (0, 0), (0, LANES - tc * RWKV_HEADS)))
    s0l = s0.transpose(0, 2, 1, 3).reshape(batch, HEAD_DIM, RWKV_WIDTH)
    ones_half, wv, wy = _scan_consts(tc)
    full = lambda a: pl.BlockSpec(a.shape, lambda p, t: (0,) * a.ndim)
    seq_spec = pl.BlockSpec((SCAN_SEQS, tt, RWKV_WIDTH), lambda p, t: (p, t, 0))
    blk_spec = pl.BlockSpec((SCAN_SEQS, tt // tc, HEAD_DIM, LANES), lambda p, t: (p, t, 0, 0))
    st_spec = pl.BlockSpec((SCAN_SEQS, HEAD_DIM, RWKV_WIDTH), lambda p, t: (p, 0, 0))
    y4, s_out = pl.pallas_call(
        functools.partial(_scan_kernel, tc),
        grid=(batch // SCAN_SEQS, seq // tt),
        in_specs=[seq_spec] * 5 + [blk_spec, st_spec, full(ones_half), full(wv), full(wy)],
        out_specs=(blk_spec, st_spec),
        out_shape=(jax.ShapeDtypeStruct((batch, nb, HEAD_DIM, LANES), F32),
                   jax.ShapeDtypeStruct((batch, HEAD_DIM, RWKV_WIDTH), F32)),
        scratch_shapes=[pltpu.VMEM((SCAN_SEQS, tc, HEAD_DIM, LANES), F32)],
        compiler_params=_cparams(("arbitrary", "arbitrary")),
        name="rwkv_scan",
    )(to3(r), to3(w), to3(k), to3(kk), to3(b), vts, s0l, ones_half, wv, wy)
    y = y4[..., :tc * RWKV_HEADS].reshape(batch, nb, HEAD_DIM, tc, RWKV_HEADS).transpose(0, 1, 3, 4, 2)
    s_fin = s_out.reshape(batch, HEAD_DIM, RWKV_HEADS, HEAD_DIM).transpose(0, 2, 1, 3)
    return y.reshape(batch * seq, RWKV_WIDTH), s_fin


def _out_kernel(x_ref, nsa_ref, y_ref, bonus_ref, g_ref, gt_ref, lnw_ref, lnb_ref, ones_ref, wo_ref, gp_ref, o_ref):
    y = y_ref[...]
    ones = ones_ref[...]
    inv = 1.0 / HEAD_DIM
    yc = y - _dot_hi(y, ones) * inv
    var = _dot_hi(yc * yc, ones) * inv
    yn = yc * lax.rsqrt(var + GN_EPS) * lnw_ref[...] + lnb_ref[...]
    mr = ((yn + bonus_ref[...]) * g_ref[...]).astype(BF16)
    z = _dot(nsa_ref[...].astype(BF16), wo_ref[:NSA_WIDTH, :]) + _dot(mr, wo_ref[NSA_WIDTH:, :])
    zn = z * lax.rsqrt(jnp.mean(z * z, axis=-1, keepdims=True) + NORM_EPS) * gp_ref[...]
    o_ref[...] = x_ref[...] + gt_ref[...] * zn


def _mixer_out(x, nsa, y, bonus, g, mod, tokens_per_row, ln_w, ln_b, w_out, g_post, tm):
    n = x.shape[0]
    ones = jnp.asarray(_head_ones(RWKV_WIDTH))
    wo = w_out.astype(BF16)
    row1 = lambda a: a.reshape(1, -1)
    full = lambda a: pl.BlockSpec(a.shape, lambda i: (0,) * a.ndim)
    half = pl.BlockSpec((tm, RWKV_WIDTH), lambda i: (i, 0))
    xspec = pl.BlockSpec((tm, D_MODEL), lambda i: (i, 0))
    lw, lb, gp = row1(ln_w), row1(ln_b), row1(g_post)
    return pl.pallas_call(
        _out_kernel,
        grid=(n // tm,),
        in_specs=[xspec, half, half, half, half, _mod_specs(mod, tm, tokens_per_row, 2),
                  full(lw), full(lb), full(ones), full(wo), full(gp)],
        out_specs=xspec,
        out_shape=jax.ShapeDtypeStruct((n, D_MODEL), F32),
        compiler_params=_cparams(("arbitrary",)),
        name="mixer_out",
    )(x, nsa, y, bonus, g, mod, lw, lb, ones, wo, gp)


def _staircase():
    return [(a, b) for a in range(PEER_TOPK) for b in range(PEER_TOPK) if (a + 1) * (b + 1) <= PEER_TOPK]


def _top16_ranked(s):
    rows = s.shape[0]
    idx_row = lax.broadcasted_iota(jnp.int32, s.shape, 0)
    rank = jnp.full(s.shape, 99.0, F32)
    vals = []
    for it in range(PEER_TOPK):
        m = jnp.max(s, axis=0, keepdims=True)
        idx = jnp.min(jnp.where(s == m, idx_row, rows), axis=0, keepdims=True)
        hit = idx_row == idx
        rank = jnp.where(hit, np.float32(it), rank)
        s = jnp.where(hit, BELOW_NEG_INF, s)
        vals.append(m)
    return rank, vals


PEER_STRIP = 128
PEER_EXPERT_BLOCK = 512


def _peer_route(hT, wqT_ref, keys_ref, hd):
    n = hT.shape[1]
    stairs = _staircase()
    n_cand = -(-len(stairs) // SUBLANES) * SUBLANES
    cand_row = lax.broadcasted_iota(jnp.int32, (n_cand, n), 0)
    ranks, vals = [], []
    for c in range(2):
        q = _dot(wqT_ref[(hd * 2 + c) * PEER_HALF:(hd * 2 + c + 1) * PEER_HALF, :], hT)
        q = q * lax.rsqrt(jnp.mean(q * q, axis=0, keepdims=True) + NORM_EPS)
        s = _dot_hi(keys_ref[hd * 2 + c], q)
        rank, top = _top16_ranked(s)
        ranks.append((rank, s))
        vals.append(top)
    cand = jnp.concatenate([vals[0][a] + vals[1][b] for a, b in stairs]
                           + [jnp.full((n_cand - len(stairs), n), BELOW_NEG_INF, F32)], axis=0)
    chosen = jnp.zeros(cand.shape, F32)
    work = cand
    for _ in range(PEER_TOPK):
        m = jnp.max(work, axis=0, keepdims=True)
        idx = jnp.min(jnp.where(work == m, cand_row, n_cand), axis=0, keepdims=True)
        hit = cand_row == idx
        chosen = jnp.where(hit, 1.0, chosen)
        work = jnp.where(hit, BELOW_NEG_INF, work)
    top_sum = vals[0][0] + vals[1][0]
    z = jnp.sum(chosen * jnp.exp(jnp.where(chosen > 0.5, cand, top_sum) - top_sum), axis=0, keepdims=True)
    rank1, s1 = ranks[0]
    rank2, s2 = ranks[1]
    cnt = jnp.zeros(rank1.shape, F32)
    for a in range(PEER_TOPK):
        rows_a = [r for r, (ka, _) in enumerate(stairs) if ka == a]
        c_a = chosen[rows_a[0]:rows_a[0] + 1]
        for r in rows_a[1:]:
            c_a = c_a + chosen[r:r + 1]
        cnt = jnp.where(rank1 == np.float32(a), c_a, cnt)
    return jnp.exp(s1 - vals[0][0]) / z, cnt, jnp.exp(s2 - vals[1][0]), rank2


def _peer_kernel(x_ref, g2_ref, sc_ref, sh_ref, gt_ref, gp_ref, wqT_ref, keys_ref, u_ref, vT_ref,
                 o_ref, hT_ref, a_ref, cnt_ref, b_ref, r2_ref, acc_ref):
    i = pl.program_id(1)
    n = x_ref.shape[0]
    strip = min(PEER_STRIP, n)

    @pl.when(i == 0)
    def _():
        x = x_ref[...]
        h = x * lax.rsqrt(jnp.mean(x * x, axis=-1, keepdims=True) + NORM_EPS) * g2_ref[...]
        hT_ref[...] = (h * (1.0 + sc_ref[...]) + sh_ref[...]).T.astype(BF16)
        acc_ref[...] = jnp.zeros(acc_ref.shape, F32)
        for st in range(n // strip):
            lanes = slice(st * strip, (st + 1) * strip)
            hT = hT_ref[:, lanes]
            for hd in range(PEER_HEADS):
                a, cnt, bcol, r2 = _peer_route(hT, wqT_ref, keys_ref, hd)
                a_ref[hd, :, lanes] = a
                cnt_ref[hd, :, lanes] = cnt
                b_ref[hd, :, lanes] = bcol
                r2_ref[hd, :, lanes] = r2

    act = _gelu(_dot(u_ref[...], hT_ref[...]))
    blocks = u_ref.shape[0] // PEER_KEYS
    weighted = []
    for j in range(blocks):
        wgt = jnp.zeros((PEER_KEYS, n), F32)
        for hd in range(PEER_HEADS):
            a_row = a_ref[hd, pl.ds(i * blocks + j, 1), :]
            c_row = cnt_ref[hd, pl.ds(i * blocks + j, 1), :]
            wgt = wgt + jnp.where(r2_ref[hd] < c_row, b_ref[hd] * a_row, 0.0)
        weighted.append((wgt * act[j * PEER_KEYS:(j + 1) * PEER_KEYS]).astype(BF16))
    acc_ref[...] += _dot(vT_ref[...], jnp.concatenate(weighted, axis=0))

    @pl.when(i == pl.num_programs(1) - 1)
    def _():
        z = acc_ref[...].T
        zn = z * lax.rsqrt(jnp.mean(z * z, axis=-1, keepdims=True) + NORM_EPS) * gp_ref[...]
        o_ref[...] = x_ref[...] + gt_ref[...] * zn


def _peer_layer(x, g_pre, mod, tokens_per_row, g_post, w_q, sub_keys, u_bf, vT_bf, tm):
    n = x.shape[0]
    wqT = w_q.astype(BF16).T
    keys = sub_keys.reshape(PEER_HEADS * 2, PEER_KEYS, PEER_HALF)
    eb = PEER_EXPERT_BLOCK
    row1 = lambda a: a.reshape(1, -1)
    g2, gp = row1(g_pre), row1(g_post)
    full = lambda a: pl.BlockSpec(a.shape, lambda t, i: (0,) * a.ndim)
    modspec = lambda col: _mod_specs(mod, tm, tokens_per_row, col)
    xspec = pl.BlockSpec((tm, D_MODEL), lambda t, i: (t, 0))
    per_head = pltpu.VMEM((PEER_HEADS, PEER_KEYS, tm), F32)
    return pl.pallas_call(
        _peer_kernel,
        grid=(n // tm, u_bf.shape[0] // eb),
        in_specs=[xspec, full(g2), modspec(4), modspec(3), modspec(5), full(gp), full(wqT), full(keys),
                  pl.BlockSpec((eb, D_MODEL), lambda t, i: (i, 0)),
                  pl.BlockSpec((D_MODEL, eb), lambda t, i: (0, i))],
        out_specs=xspec,
        out_shape=jax.ShapeDtypeStruct((n, D_MODEL), F32),
        scratch_shapes=[pltpu.VMEM((D_MODEL, tm), BF16), per_head, per_head, per_head, per_head,
                        pltpu.VMEM((D_MODEL, tm), F32)],
        compiler_params=_cparams(("arbitrary", "arbitrary")),
        name="peer",
    )(x, g2, mod, mod, mod, gp, wqT, keys, u_bf, vT_bf)


def kernel(x_prompt, x_sample, c_prompt, c_sample, cache_nsa, page_table, cache_win, state_wkv, state_shift, w_ada, b_ada, norm_pre1, norm_post1, norm_pre2, norm_post2, w_in, w_out, cmp_k_w1, cmp_k_w2, cmp_k_pe, cmp_v_w1, cmp_v_w2, cmp_v_pe, rwkv_mu, rwkv_w0, rwkv_w2, rwkv_a0, rwkv_a2, rwkv_g2, rwkv_k_k, rwkv_k_a, rwkv_r_k, rwkv_ln_w, rwkv_ln_b, peer_w_q, peer_sub_keys, peer_u, peer_v):
    batch, seq, d = x_prompt.shape
    dec_batch, dec_seq, _ = x_sample.shape
    depth = w_ada.shape[0]
    assert d == D_MODEL and seq % WINDOW == 0 and seq & (seq - 1) == 0 and dec_seq & (dec_seq - 1) == 0
    assert batch % SCAN_SEQS == 0 and dec_batch % SCAN_SEQS == 0
    n_p, n_s = batch * seq, dec_batch * dec_seq
    tile = lambda n: min(256, n)
    xp = x_prompt.reshape(n_p, d)
    xs = x_sample.reshape(n_s, d)
    c_all = jnp.concatenate([c_prompt, c_sample], axis=0)
    c_all = jnp.pad(c_all, ((0, -c_all.shape[0] % SUBLANES), (0, 0)))
    w_keep = min(WINDOW, seq)
    outs = [[] for _ in range(8)]
    for l in range(depth):
        mod = _modulation(c_all, w_ada[l], b_ada[l])
        mod_p = mod[:batch].reshape(batch, 1, 6 * d)
        mod_s = jnp.repeat(mod[batch:batch + dec_batch], dec_seq, axis=0)
        kw = _compress_weights(cmp_k_w1[l], cmp_k_w2[l], cmp_k_pe[l])
        vw = _compress_weights(cmp_v_w1[l], cmp_v_w2[l], cmp_v_pe[l])
        rwkv_w = (rwkv_mu[l], rwkv_w0[l], rwkv_w2[l], rwkv_a0[l], rwkv_a2[l], rwkv_g2[l],
                  rwkv_k_k[l], rwkv_k_a[l], rwkv_r_k[l], rwkv_ln_w[l], rwkv_ln_b[l])
        u_bf = peer_u[l].astype(BF16)
        vT_bf = peer_v[l].astype(BF16).T

        qT, kv, kvb, kvT, gT, rw = _in_projection(xp, norm_pre1[l], mod_p, seq, w_in[l], tile(n_p))
        kc, vcT, kbias, vbias = _compress_prompt(kv, batch, seq, kw, vw)
        nsa = _nsa_prompt(qT, kvb, kvT, gT, kc, vcT, batch, seq)
        r, w, k, kk, b, v, g, bonus = _rwkv_pre(rw, jnp.zeros((batch, 1, RWKV_PROJ), F32), seq, tile(n_p), rwkv_w)
        y, wkv_p = _rwkv_scan(r, w, k, kk, b, v, jnp.zeros((batch, RWKV_HEADS, HEAD_DIM, HEAD_DIM), F32),
                              batch, seq, min(256, seq))
        x1 = _mixer_out(xp, nsa, y, bonus, g, mod_p, seq, rwkv_ln_w[l], rwkv_ln_b[l], w_out[l], norm_post1[l], tile(n_p))
        xp = _peer_layer(x1, norm_pre2[l], mod_p, seq, norm_post2[l], peer_w_q[l], peer_sub_keys[l], u_bf, vT_bf, tile(n_p))
        kv3 = kv.reshape(batch, seq, KV6)
        outs[0].append(kv3[:, :, :4 * KV_WIDTH].reshape(batch, seq, 4, NSA_KV_HEADS, HEAD_DIM))
        outs[2].append(kv3[:, seq - w_keep:, 4 * KV_WIDTH:].reshape(batch, w_keep, 2, NSA_KV_HEADS, HEAD_DIM))
        outs[4].append(wkv_p)
        outs[6].append(rw.reshape(batch, seq, RWKV_PROJ)[:, -1])

        qT, kv, _, _, gT, rw = _in_projection(xs, norm_pre1[l], mod_s, None, w_in[l], tile(n_s))
        cacheT = jnp.transpose(cache_nsa[l], (0, 2, 3, 4, 1)).reshape(cache_nsa.shape[1], CACHE_CHANNELS, cache_nsa.shape[2])
        winT = jnp.transpose(cache_win[l], (0, 2, 3, 4, 1)).reshape(dec_batch, 2 * KV_WIDTH, cache_win.shape[2])
        nsa = _nsa_sample(qT.T, gT, kv, cacheT, page_table, winT, kw, vw, kbias, vbias, dec_batch, dec_seq)
        shift_tok = jnp.repeat(state_shift[l], dec_seq, axis=0)
        r, w, k, kk, b, v, g, bonus = _rwkv_pre(rw, shift_tok, dec_seq, tile(n_s), rwkv_w)
        y, wkv_s = _rwkv_scan(r, w, k, kk, b, v, state_wkv[l], dec_batch, dec_seq, dec_seq)
        x1 = _mixer_out(xs, nsa, y, bonus, g, mod_s, None, rwkv_ln_w[l], rwkv_ln_b[l], w_out[l], norm_post1[l], tile(n_s))
        xs = _peer_layer(x1, norm_pre2[l], mod_s, None, norm_post2[l], peer_w_q[l], peer_sub_keys[l], u_bf, vT_bf, tile(n_s))
        kv3 = kv.reshape(dec_batch, dec_seq, KV6)
        outs[1].append(kv3[:, :, :4 * KV_WIDTH].reshape(dec_batch, dec_seq, 4, NSA_KV_HEADS, HEAD_DIM))
        new_win = kv3[:, :, 4 * KV_WIDTH:].reshape(dec_batch, dec_seq, 2, NSA_KV_HEADS, HEAD_DIM)
        outs[3].append(jnp.concatenate([cache_win[l], new_win], axis=1)[:, dec_seq:])
        outs[5].append(wkv_s)
        outs[7].append(rw.reshape(dec_batch, dec_seq, RWKV_PROJ)[:, -1])
    return (xp.reshape(batch, seq, d), xs.reshape(dec_batch, dec_seq, d)) + tuple(jnp.stack(o) for o in outs)
```

```python
import functools

import numpy as np
import jax
import jax.numpy as jnp
from jax import lax
from jax.experimental import pallas as pl
from jax.experimental.pallas import tpu as pltpu

F32 = jnp.float32
BF16 = jnp.bfloat16
HIGHEST = lax.Precision.HIGHEST

LANES = 128
SUBLANES = 8
VMEM_LIMIT_BYTES = 56 * 1024 * 1024

D_MODEL = 1024
HEAD_DIM = 64
NORM_EPS = 1e-6
NEG_INF = -1e30
BELOW_NEG_INF = -3e38
NSA_HEADS = 8
NSA_KV_HEADS = 2
NSA_GROUP = NSA_HEADS // NSA_KV_HEADS
NSA_WIDTH = NSA_HEADS * HEAD_DIM
KV_WIDTH = NSA_KV_HEADS * HEAD_DIM
CMP_STRIDE = 16
CMP_LEN = 32
CMP_HIDDEN = 2 * HEAD_DIM
SEL_BLOCK = 64
SEL_TOPK = 16
WINDOW = 512
Q_BLOCK = 128
FORCE_BONUS = 1e4
ATTN_SCALE = HEAD_DIM ** -0.5
RWKV_HEADS = 8
RWKV_WIDTH = RWKV_HEADS * HEAD_DIM
DECAY_LORA = 64
ICLR_LORA = 64
GATE_LORA = 128
GN_EPS = 64e-5
NSA_PROJ = NSA_WIDTH + 6 * KV_WIDTH + 3 * NSA_HEADS
RWKV_PROJ = 3 * RWKV_WIDTH + DECAY_LORA + ICLR_LORA + GATE_LORA
KV6 = 6 * KV_WIDTH
GATE_ROWS = 32
PEER_HEADS = 8
PEER_KEYS = 128
PEER_HALF = 128
PEER_TOPK = 16
KEY_CHUNK = 128
SEL_CHUNK = 256


def _cparams(sem):
    return pltpu.CompilerParams(dimension_semantics=sem, vmem_limit_bytes=VMEM_LIMIT_BYTES)


def _gelu(x):
    return 0.5 * x * (1.0 + lax.erf(x * np.float32(0.7071067811865476)))


def _dot(a, b):
    return jnp.dot(a, b, preferred_element_type=F32)


def _dot_nt(a, b):
    return lax.dot_general(a, b, (((1,), (1,)), ((), ())), preferred_element_type=F32)


def _dot_hi(a, b):
    return jnp.dot(a, b, preferred_element_type=F32, precision=HIGHEST)


def _mod_kernel(c_ref, w_ref, b_ref, o_ref):
    c = c_ref[...]
    o_ref[...] = _dot_hi(c * jax.nn.sigmoid(c), w_ref[...]) + b_ref[...]


def _modulation(c, w_ada, b_ada):
    rows, d = c.shape
    n = w_ada.shape[1]
    tn = 1536
    return pl.pallas_call(
        _mod_kernel,
        grid=(n // tn,),
        in_specs=[pl.BlockSpec((rows, d), lambda j: (0, 0)),
                  pl.BlockSpec((d, tn), lambda j: (0, j)),
                  pl.BlockSpec((1, tn), lambda j: (0, j))],
        out_specs=pl.BlockSpec((rows, tn), lambda j: (0, j)),
        out_shape=jax.ShapeDtypeStruct((rows, n), F32),
        compiler_params=_cparams(("arbitrary",)),
        name="modulation",
    )(c, w_ada, b_ada.reshape(1, n))


def _inproj_kernel(x_ref, g_ref, sc_ref, sh_ref, wqT_ref, wkv_ref, wkvT_ref, wgT_ref, wr_ref,
                   qT_ref, kv_ref, kvb_ref, kvT_ref, gT_ref, rw_ref):
    x = x_ref[...]
    h = x * lax.rsqrt(jnp.mean(x * x, axis=-1, keepdims=True) + NORM_EPS) * g_ref[...]
    h = (h * (1.0 + sc_ref[...]) + sh_ref[...]).astype(BF16)
    qT_ref[...] = _dot_nt(wqT_ref[...], h)
    kv = _dot(h, wkv_ref[...])
    kv_ref[...] = kv
    kvb_ref[...] = kv.astype(BF16)
    kvT_ref[...] = _dot_nt(wkvT_ref[...], h).astype(BF16)
    gT_ref[...] = jax.nn.sigmoid(_dot_nt(wgT_ref[...], h))
    rw_ref[...] = _dot(h, wr_ref[...])


def _mod_specs(mod, tm, tokens_per_row, col):
    if tokens_per_row is None:
        return pl.BlockSpec((tm, D_MODEL), lambda i, *_: (i, col))
    tiles_per_row = tokens_per_row // tm
    return pl.BlockSpec((None, 1, D_MODEL), lambda i, *_: (i // tiles_per_row, 0, col))


def _in_projection(x, g_pre, mod, tokens_per_row, w_in, tm):
    n = x.shape[0]
    wb = w_in.astype(BF16)
    wq = wb[:, :NSA_WIDTH]
    wkv = wb[:, NSA_WIDTH:NSA_WIDTH + KV6]
    wg = jnp.pad(wb[:, NSA_WIDTH + KV6:NSA_PROJ], ((0, 0), (0, GATE_ROWS - 3 * NSA_HEADS)))
    wr = wb[:, NSA_PROJ:]
    full = lambda a: pl.BlockSpec(a.shape, lambda i: (0,) * a.ndim)
    wqT, wkvT, wgT = wq.T, wkv.T, wg.T
    g2 = g_pre.reshape(1, D_MODEL)
    outs = (
        jax.ShapeDtypeStruct((NSA_WIDTH, n), F32),
        jax.ShapeDtypeStruct((n, KV6), F32),
        jax.ShapeDtypeStruct((n, KV6), BF16),
        jax.ShapeDtypeStruct((KV6, n), BF16),
        jax.ShapeDtypeStruct((GATE_ROWS, n), F32),
        jax.ShapeDtypeStruct((n, RWKV_PROJ), F32),
    )
    return pl.pallas_call(
        _inproj_kernel,
        grid=(n // tm,),
        in_specs=[pl.BlockSpec((tm, D_MODEL), lambda i: (i, 0)), full(g2),
                  _mod_specs(mod, tm, tokens_per_row, 1), _mod_specs(mod, tm, tokens_per_row, 0),
                  full(wqT), full(wkv), full(wkvT), full(wgT), full(wr)],
        out_specs=(pl.BlockSpec((NSA_WIDTH, tm), lambda i: (0, i)),
                   pl.BlockSpec((tm, KV6), lambda i: (i, 0)),
                   pl.BlockSpec((tm, KV6), lambda i: (i, 0)),
                   pl.BlockSpec((KV6, tm), lambda i: (0, i)),
                   pl.BlockSpec((GATE_ROWS, tm), lambda i: (0, i)),
                   pl.BlockSpec((tm, RWKV_PROJ), lambda i: (i, 0))),
        out_shape=outs,
        compiler_params=_cparams(("arbitrary",)),
        name="in_projection",
    )(x, g2, mod, mod, wqT, wkv, wkvT, wgT, wr)


def _compress_weights(w1, w2, pe):
    z = jnp.zeros((CMP_STRIDE, HEAD_DIM, CMP_HIDDEN), F32)
    a, b = w1[:CMP_STRIDE], w1[CMP_STRIDE:]
    top = jnp.concatenate([a, z, b, z], axis=-1)
    bot = jnp.concatenate([z, a, z, b], axis=-1)
    wbd = jnp.concatenate([top, bot], axis=1)
    half = CMP_STRIDE // 2
    wbd = jnp.concatenate([wbd[:half], wbd[half:]], axis=1).astype(BF16)
    z2 = jnp.zeros((CMP_HIDDEN, HEAD_DIM), F32)
    w2bd = jnp.concatenate([jnp.concatenate([w2, z2], axis=1),
                            jnp.concatenate([z2, w2], axis=1)], axis=0).astype(BF16)
    return wbd, w2bd, w2bd.T, pe.reshape(1, CMP_LEN * HEAD_DIM), w1.reshape(CMP_LEN * HEAD_DIM, CMP_HIDDEN)


def _compress_bias(pe_ref, w1_ref):
    b = _dot_hi(jnp.broadcast_to(pe_ref[...], (SUBLANES, CMP_LEN * HEAD_DIM)), w1_ref[...])[0:1]
    return jnp.concatenate([b, b], axis=1)


def _compress_hidden(rows_at, n_chunks, wbd_ref, bias):
    acc = jnp.zeros((n_chunks, 4 * CMP_HIDDEN), F32)
    half = CMP_STRIDE // 2
    for l in range(half):
        x = jnp.concatenate([rows_at(l), rows_at(l + half)], axis=1).astype(BF16)
        acc = acc + _dot(x, wbd_ref[l])
    first, second = acc[:, :2 * CMP_HIDDEN], acc[:, 2 * CMP_HIDDEN:]
    hid = first + pltpu.roll(second, n_chunks - 1, 0) + bias
    return _gelu(hid).astype(BF16)


def _cmp_prompt_kernel(k_ref, v_ref, kwbd_ref, kw2_ref, kpe_ref, kw1_ref, vwbd_ref, vw2T_ref, vpe_ref, vw1_ref,
                       kc_ref, vcT_ref, kb_ref, vb_ref):
    n_chunks = k_ref.shape[0] // CMP_STRIDE
    kb = _compress_bias(kpe_ref, kw1_ref)
    vb = _compress_bias(vpe_ref, vw1_ref)
    kb_ref[...] = kb
    vb_ref[...] = vb
    gk = _compress_hidden(lambda l: k_ref[pl.ds(l, n_chunks, stride=CMP_STRIDE), :], n_chunks, kwbd_ref, kb)
    kc_ref[...] = _dot(gk, kw2_ref[...]).astype(BF16)
    gv = _compress_hidden(lambda l: v_ref[pl.ds(l, n_chunks, stride=CMP_STRIDE), :], n_chunks, vwbd_ref, vb)
    vcT_ref[...] = _dot_nt(vw2T_ref[...], gv).astype(BF16)


def _compress_prompt(kv, batch, seq, kw, vw):
    nc = seq // CMP_STRIDE
    kwbd, kw2, _, kpe, kw1 = kw
    vwbd, _, vw2T, vpe, vw1 = vw
    full = lambda a: pl.BlockSpec(a.shape, lambda b: (0,) * a.ndim)
    return pl.pallas_call(
        _cmp_prompt_kernel,
        grid=(batch,),
        in_specs=[pl.BlockSpec((seq, KV_WIDTH), lambda b: (b, 0)),
                  pl.BlockSpec((seq, KV_WIDTH), lambda b: (b, 1)),
                  full(kwbd), full(kw2), full(kpe), full(kw1), full(vwbd), full(vw2T), full(vpe), full(vw1)],
        out_specs=(pl.BlockSpec((None, nc, KV_WIDTH), lambda b: (b, 0, 0)),
                   pl.BlockSpec((None, KV_WIDTH, nc), lambda b: (b, 0, 0)),
                   pl.BlockSpec((1, 2 * CMP_HIDDEN), lambda b: (0, 0)),
                   pl.BlockSpec((1, 2 * CMP_HIDDEN), lambda b: (0, 0))),
        out_shape=(jax.ShapeDtypeStruct((batch, nc, KV_WIDTH), BF16),
                   jax.ShapeDtypeStruct((batch, KV_WIDTH, nc), BF16),
                   jax.ShapeDtypeStruct((1, 2 * CMP_HIDDEN), F32),
                   jax.ShapeDtypeStruct((1, 2 * CMP_HIDDEN), F32)),
        compiler_params=_cparams(("arbitrary",)),
        name="compress_prompt",
    )(kv, kv, kwbd, kw2, kpe, kw1, vwbd, vw2T, vpe, vw1)


def _overlap_T(n_sel_rows, n_cmp_cols, n_sel, n_cmp):
    s = np.arange(n_sel_rows)[:, None] * SEL_BLOCK
    c = np.arange(n_cmp_cols)[None, :] * CMP_STRIDE
    ov = (c < s + SEL_BLOCK) & (c + CMP_LEN > s)
    ov &= (np.arange(n_sel_rows)[:, None] < n_sel) & (np.arange(n_cmp_cols)[None, :] < n_cmp)
    return jnp.asarray(ov.astype(np.float32))


def _masked_softmax_cols(s, mask):
    s = jnp.where(mask, s, NEG_INF)
    m = jnp.max(s, axis=0, keepdims=True)
    e = jnp.where(mask, jnp.exp(s - m), 0.0)
    return e / jnp.maximum(jnp.sum(e, axis=0, keepdims=True), 1e-30)


def _select_blocks(imp, q_pos, n_sel):
    rows = imp.shape[0]
    blk = lax.broadcasted_iota(jnp.int32, imp.shape, 0)
    cur = jnp.right_shift(q_pos, 6)
    valid = blk <= cur
    forced = (blk == 0) | (blk == cur) | (blk == cur - 1)
    score = jnp.where(valid, imp + jnp.where(forced, FORCE_BONUS, 0.0), NEG_INF)
    score = jnp.where(blk < n_sel, score, BELOW_NEG_INF)
    sel = jnp.zeros(imp.shape, F32)
    for _ in range(min(SEL_TOPK, n_sel)):
        m = jnp.max(score, axis=0, keepdims=True)
        idx = jnp.min(jnp.where(score == m, blk, rows), axis=0, keepdims=True)
        hit = blk == idx
        sel = jnp.where(hit & (m > 0.5 * NEG_INF), 1.0, sel)
        score = jnp.where(hit, BELOW_NEG_INF, score)
    return sel


def _online_update(carry, s, mask, weighted_values):
    m, l, acc = carry
    s = jnp.where(mask, s, NEG_INF)
    m_new = jnp.maximum(m, jnp.max(s, axis=0, keepdims=True))
    alpha = jnp.exp(m - m_new)
    e = jnp.where(mask, jnp.exp(s - m_new), 0.0)
    l = alpha * l + jnp.sum(e, axis=0, keepdims=True)
    acc = alpha * acc + weighted_values(e.astype(BF16))
    return m_new, l, acc


def _online_update_biased(carry, s, weighted_values):
    m, l, acc = carry
    m_new = jnp.maximum(m, jnp.max(s, axis=0, keepdims=True))
    alpha = jnp.exp(m - m_new)
    e = jnp.exp(s - m_new)
    l = alpha * l + jnp.sum(e, axis=0, keepdims=True)
    acc = alpha * acc + weighted_values(e.astype(BF16))
    return m_new, l, acc


def _online_init(rows, cols):
    return (jnp.full((1, cols), NEG_INF, F32), jnp.zeros((1, cols), F32), jnp.zeros((rows, cols), F32))


def _online_finish(carry):
    _, l, acc = carry
    return acc / jnp.maximum(l, 1e-30)


def _nsa_prompt_kernel(qT_ref, ksel_ref, kwin_ref, vselT_ref, vwinT_ref, kc_ref, vcT_ref, gT_ref, ovT_ref,
                       o_ref, sel_ref):
    qb = pl.program_id(1)
    n_cmp_rows = kc_ref.shape[0]
    n_sel = ovT_ref.shape[0]
    cols = NSA_GROUP * Q_BLOCK
    pos0 = qb * Q_BLOCK
    groups = range(NSA_KV_HEADS)
    qT = qT_ref[...] * ATTN_SCALE
    gates = gT_ref[...]
    q_pos1 = pos0 + lax.broadcasted_iota(jnp.int32, (1, Q_BLOCK), 1)
    q_pos = jnp.concatenate([q_pos1] * NSA_GROUP, axis=1)
    key_row = lax.broadcasted_iota(jnp.int32, (KEY_CHUNK, cols), 0)
    sel_row = lax.broadcasted_iota(jnp.int32, (SEL_CHUNK, cols), 0)
    zeros_half = jnp.zeros((HEAD_DIM, cols), F32)
    rows_of = lambda g: slice(g * HEAD_DIM, (g + 1) * HEAD_DIM)

    qbd, o_c = [], []
    for g in groups:
        qg = jnp.concatenate([qT[(g * NSA_GROUP + r) * HEAD_DIM:(g * NSA_GROUP + r + 1) * HEAD_DIM, :]
                              for r in range(NSA_GROUP)], axis=1)
        qbd.append(jnp.concatenate([qg, zeros_half] if g == 0 else [zeros_half, qg], axis=0).astype(BF16))
        s = _dot(kc_ref[...], qbd[g])
        n_idx = lax.broadcasted_iota(jnp.int32, s.shape, 0)
        cmask = (n_idx * CMP_STRIDE + (CMP_LEN - 1) <= q_pos) & (n_idx < n_cmp_rows - 1)
        p = _masked_softmax_cols(s, cmask)
        o_c.append(_dot(vcT_ref[rows_of(g), :], p.astype(BF16)))
        p_sum = p[:, 0:Q_BLOCK]
        for r in range(1, NSA_GROUP):
            p_sum = p_sum + p[:, r * Q_BLOCK:(r + 1) * Q_BLOCK]
        imp = _dot_hi(ovT_ref[...], p_sum)
        sel = _select_blocks(imp, q_pos1, n_sel)
        sel_ref[g] = jnp.concatenate([jnp.where(sel > 0.5, 0.0, NEG_INF)] * NSA_GROUP, axis=1)

    def sel_scores(j, g):
        keys = pl.ds(pl.multiple_of(j * SEL_CHUNK, SEL_CHUNK), SEL_CHUNK)
        blocks_per_chunk = SEL_CHUNK // SEL_BLOCK
        bias = jnp.concatenate(
            [jnp.broadcast_to(sel_ref[g, pl.ds(j * blocks_per_chunk + i, 1), :], (SEL_BLOCK, cols))
             for i in range(blocks_per_chunk)], axis=0)
        return _dot(ksel_ref[keys, :], qbd[g]) + bias, functools.partial(_dot, vselT_ref[rows_of(g), keys])

    def sel_step(j, carry):
        return tuple(_online_update_biased(carry[g], *sel_scores(j, g)) for g in groups)
    n_full = pos0 // SEL_CHUNK
    carry = lax.fori_loop(0, n_full, sel_step, tuple(_online_init(HEAD_DIM, cols) for _ in groups))
    causal = n_full * SEL_CHUNK + sel_row <= q_pos
    o_s = []
    for g in groups:
        sc, weighted_values = sel_scores(n_full, g)
        o_s.append(_online_finish(_online_update_biased(carry[g], jnp.where(causal, sc, NEG_INF), weighted_values)))

    def win_step(i, carry):
        jj = qb - WINDOW // KEY_CHUNK + i
        keys = pl.ds(pl.multiple_of(jnp.maximum(jj, 0) * KEY_CHUNK, KEY_CHUNK), KEY_CHUNK)
        k = kwin_ref[keys, :]
        k_pos = jj * KEY_CHUNK + key_row
        mask = (k_pos <= q_pos) & (q_pos - k_pos < WINDOW) & (k_pos >= 0)
        return tuple(_online_update(carry[g], _dot(k, qbd[g]), mask, functools.partial(_dot, vwinT_ref[rows_of(g), keys]))
                     for g in groups)
    carry = lax.fori_loop(0, WINDOW // KEY_CHUNK + 1, win_step, tuple(_online_init(HEAD_DIM, cols) for _ in groups))
    o_w = [_online_finish(carry[g]) for g in groups]

    out_rows = []
    for g in groups:
        for r in range(NSA_GROUP):
            h = g * NSA_GROUP + r
            cs = slice(r * Q_BLOCK, (r + 1) * Q_BLOCK)
            out_rows.append(gates[3 * h:3 * h + 1] * o_c[g][:, cs] + gates[3 * h + 1:3 * h + 2] * o_s[g][:, cs]
                            + gates[3 * h + 2:3 * h + 3] * o_w[g][:, cs])
    o_ref[...] = jnp.concatenate(out_rows, axis=0).T


def _nsa_prompt(qT, kvb, kvT, gT, kc, vcT, batch, seq):
    nq = seq // Q_BLOCK
    nc = seq // CMP_STRIDE
    n_sel = seq // SEL_BLOCK
    ovT = _overlap_T(n_sel, nc, n_sel, nc - 1)
    cols = NSA_GROUP * Q_BLOCK
    return pl.pallas_call(
        _nsa_prompt_kernel,
        grid=(batch, nq),
        in_specs=[pl.BlockSpec((NSA_WIDTH, Q_BLOCK), lambda b, q: (0, b * nq + q)),
                  pl.BlockSpec((seq, KV_WIDTH), lambda b, q: (b, 2)),
                  pl.BlockSpec((seq, KV_WIDTH), lambda b, q: (b, 4)),
                  pl.BlockSpec((KV_WIDTH, seq), lambda b, q: (3, b)),
                  pl.BlockSpec((KV_WIDTH, seq), lambda b, q: (5, b)),
                  pl.BlockSpec((None, nc, KV_WIDTH), lambda b, q: (b, 0, 0)),
                  pl.BlockSpec((None, KV_WIDTH, nc), lambda b, q: (b, 0, 0)),
                  pl.BlockSpec((GATE_ROWS, Q_BLOCK), lambda b, q: (0, b * nq + q)),
                  pl.BlockSpec((n_sel, nc), lambda b, q: (0, 0))],
        out_specs=pl.BlockSpec((Q_BLOCK, NSA_WIDTH), lambda b, q: (b * nq + q, 0)),
        out_shape=jax.ShapeDtypeStruct((batch * seq, NSA_WIDTH), F32),
        scratch_shapes=[pltpu.VMEM((NSA_KV_HEADS, n_sel, cols), F32)],
        compiler_params=_cparams(("arbitrary", "arbitrary")),
        name="nsa_prompt",
    )(qT, kvb, kvb, kvT, kvT, kc, vcT, gT, ovT)


CACHE_PARTS = 4
CACHE_CHANNELS = CACHE_PARTS * KV_WIDTH
TRANSPOSE_CHUNK = 512


def _softmax_two(s_a, mask_a, s_b, mask_b):
    m = jnp.maximum(jnp.max(jnp.where(mask_a, s_a, NEG_INF), axis=-1, keepdims=True),
                    jnp.max(jnp.where(mask_b, s_b, NEG_INF), axis=-1, keepdims=True))
    e_a = jnp.where(mask_a, jnp.exp(s_a - m), 0.0)
    e_b = jnp.where(mask_b, jnp.exp(s_b - m), 0.0)
    total = jnp.sum(e_a, axis=-1, keepdims=True) + jnp.sum(e_b, axis=-1, keepdims=True)
    return e_a.astype(BF16), e_b.astype(BF16), 1.0 / jnp.maximum(total, 1e-30)


def _nsa_sample_kernel(dec_seq, n_sel, pt_ref, q_ref, gate_ref, newT_ref, winT_ref, cache_ref, expand_hbm,
                       kwbd_ref, kw2_ref, kb_ref, vwbd_ref, vw2_ref, vb_ref, ovT_ref, fold_ref,
                       o_ref, buf_ref, rows_ref, expand_ref, sem_ref, esem_ref):
    b = pl.program_id(0)
    nb = pl.num_programs(0)
    n_pages = pt_ref.shape[1]
    page = cache_ref.shape[2]
    p_len = n_pages * page
    n_chunks = p_len // CMP_STRIDE
    n_rows = o_ref.shape[0]
    slot = b % 2

    def page_copy(seq, i, sl):
        return pltpu.make_async_copy(cache_ref.at[pt_ref[seq, i]],
                                     buf_ref.at[sl, :, pl.ds(i * page, page)], sem_ref.at[sl])

    def start_fetch(seq, sl):
        def body(i, _):
            page_copy(seq, i, sl).start()
            return 0
        lax.fori_loop(0, n_pages, body, 0)

    expand_copy = pltpu.make_async_copy(expand_hbm, expand_ref, esem_ref.at[0])

    @pl.when(b == 0)
    def _():
        expand_copy.start()
        start_fetch(0, 0)

    @pl.when(b + 1 < nb)
    def _():
        start_fetch(b + 1, 1 - slot)

    def wait_body(i, _):
        page_copy(b, i, slot).wait()
        return 0
    lax.fori_loop(0, n_pages, wait_body, 0)

    @pl.when(b == 0)
    def _():
        expand_copy.wait()

    q_all = (q_ref[...] * ATTN_SCALE).astype(BF16)
    col = lax.broadcasted_iota(jnp.int32, (1, LANES), 1)
    q_pos = p_len + (col & (dec_seq - 1))

    def compressed(part, wbd_ref, bias_ref):
        def move(j, _):
            cols = pl.ds(pl.multiple_of(j * TRANSPOSE_CHUNK, TRANSPOSE_CHUNK), TRANSPOSE_CHUNK)
            rows_ref[cols, :] = buf_ref[slot, part * KV_WIDTH:(part + 1) * KV_WIDTH, cols].T
            return 0
        lax.fori_loop(0, p_len // TRANSPOSE_CHUNK, move, 0)
        return _compress_hidden(lambda l: rows_ref[pl.ds(l, n_chunks, stride=CMP_STRIDE), :], n_chunks,
                                wbd_ref, bias_ref[...])

    kc = _dot(compressed(0, kwbd_ref, kb_ref), kw2_ref[...]).astype(BF16)
    vc = _dot(compressed(1, vwbd_ref, vb_ref), vw2_ref[...]).astype(BF16)

    s = _dot_nt(kc, q_all)
    n_idx = lax.broadcasted_iota(jnp.int32, s.shape, 0)
    cmask = (n_idx * CMP_STRIDE + (CMP_LEN - 1) <= q_pos) & (n_idx < n_chunks - 1)
    p = _masked_softmax_cols(s, cmask)
    o_c = lax.dot_general(p.astype(BF16), vc, (((0,), (0,)), ((), ())), preferred_element_type=F32)
    imp = _dot_hi(_dot_hi(ovT_ref[...], p), fold_ref[...])
    sel = _select_blocks(imp, q_pos, n_sel)
    selT = _dot_hi(sel, fold_ref[...].T).T[:n_rows]

    q_rows = q_all[:n_rows]
    row_tok = lax.broadcasted_iota(jnp.int32, (n_rows, 1), 0) & (dec_seq - 1)
    lane = lax.broadcasted_iota(jnp.int32, (n_rows, LANES), 1)
    new_ok = (lane <= row_tok) & (lane < dec_seq)
    newT = newT_ref[...].astype(BF16)
    part = lambda a, i: a[i * KV_WIDTH:(i + 1) * KV_WIDTH]

    chosen = _dot(selT.astype(BF16), expand_ref[...]) > 0.5
    s_past = _dot(q_rows, buf_ref[slot, 2 * KV_WIDTH:3 * KV_WIDTH, :].astype(BF16))
    new_blk = p_len // SEL_BLOCK
    new_sel = (selT[:, new_blk:new_blk + 1] > 0.5) & new_ok
    e_p, e_n, inv = _softmax_two(s_past, chosen, _dot(q_rows, part(newT, 2)), new_sel)
    o_s = (_dot_nt(e_p, buf_ref[slot, 3 * KV_WIDTH:4 * KV_WIDTH, :].astype(BF16)) + _dot_nt(e_n, part(newT, 3))) * inv

    winT = winT_ref[...].astype(BF16)
    w_buf = winT.shape[1]
    k_pos = p_len - w_buf + lax.broadcasted_iota(jnp.int32, (n_rows, w_buf), 1)
    wmask = (p_len + row_tok - k_pos < WINDOW) & (k_pos >= 0)
    e_w, e_n, inv = _softmax_two(_dot(q_rows, winT[:KV_WIDTH]), wmask, _dot(q_rows, part(newT, 4)), new_ok)
    o_w = (_dot_nt(e_w, winT[KV_WIDTH:]) + _dot_nt(e_n, part(newT, 5))) * inv

    o_ref[...] = gate_ref[0] * o_c[:n_rows] + gate_ref[1] * o_s + gate_ref[2] * o_w


def _nsa_sample(q, gT, kv_new, cacheT, page_table, winT, kw, vw, kbias, vbias, dec_batch, dec_seq):
    n_pages = page_table.shape[1]
    page = cacheT.shape[2]
    p_len = n_pages * page
    n_chunks = p_len // CMP_STRIDE
    n_sel = -(-(p_len + dec_seq) // SEL_BLOCK)
    sel_rows = -(-n_sel // LANES) * LANES
    n_rows = NSA_HEADS * dec_seq
    q5 = q.reshape(dec_batch, dec_seq, NSA_KV_HEADS, NSA_GROUP, HEAD_DIM).transpose(0, 2, 3, 1, 4)
    eye = jnp.eye(NSA_KV_HEADS, dtype=F32)
    qrows = (q5[:, :, :, :, None, :] * eye[None, :, None, None, :, None]).reshape(dec_batch, n_rows, KV_WIDTH)
    qrows = jnp.pad(qrows, ((0, 0), (0, LANES - n_rows), (0, 0)))
    g4 = gT[:3 * NSA_HEADS].reshape(NSA_HEADS, 3, dec_batch, dec_seq).transpose(2, 1, 0, 3).reshape(dec_batch, 3, n_rows)
    gexp = jnp.broadcast_to(g4[..., None], (dec_batch, 3, n_rows, LANES))
    newT = jnp.pad(kv_new.reshape(dec_batch, dec_seq, KV6).transpose(0, 2, 1), ((0, 0), (0, 0), (0, LANES - dec_seq)))
    ovT = _overlap_T(sel_rows, n_chunks, n_sel, n_chunks - 1)
    fold = np.zeros((LANES, LANES), np.float32)
    for g in range(NSA_KV_HEADS):
        for r in range(NSA_GROUP):
            for s in range(dec_seq):
                fold[(g * NSA_GROUP + r) * dec_seq + s, g * dec_seq + s] = 1.0
    fold = jnp.asarray(fold)
    expand = jnp.asarray(np.arange(sel_rows)[:, None] == (np.arange(p_len) // SEL_BLOCK)[None, :]).astype(BF16)
    kwbd, kw2, _, _, _ = kw
    vwbd, vw2, _, _, _ = vw
    full = lambda a: pl.BlockSpec(a.shape, lambda b, pt: (0,) * a.ndim)
    per_seq = lambda a: pl.BlockSpec((None,) + a.shape[1:], lambda b, pt: (b,) + (0,) * (a.ndim - 1))
    hbm = pl.BlockSpec(memory_space=pl.ANY)
    grid_spec = pltpu.PrefetchScalarGridSpec(
        num_scalar_prefetch=1,
        grid=(dec_batch,),
        in_specs=[per_seq(qrows), per_seq(gexp), per_seq(newT), per_seq(winT), hbm, hbm,
                  full(kwbd), full(kw2), full(kbias), full(vwbd), full(vw2), full(vbias), full(ovT), full(fold)],
        out_specs=pl.BlockSpec((None, n_rows, KV_WIDTH), lambda b, pt: (b, 0, 0)),
        scratch_shapes=[pltpu.VMEM((2, CACHE_CHANNELS, p_len), F32),
                        pltpu.VMEM((p_len, KV_WIDTH), F32),
                        pltpu.VMEM((sel_rows, p_len), BF16),
                        pltpu.SemaphoreType.DMA((2,)),
                        pltpu.SemaphoreType.DMA((1,))])
    o = pl.pallas_call(
        functools.partial(_nsa_sample_kernel, dec_seq, n_sel),
        grid_spec=grid_spec,
        out_shape=jax.ShapeDtypeStruct((dec_batch, n_rows, KV_WIDTH), F32),
        compiler_params=_cparams(("arbitrary",)),
        name="nsa_sample",
    )(page_table, qrows, gexp, newT, winT, cacheT, expand, kwbd, kw2, kbias, vwbd, vw2, vbias, ovT, fold)
    o = o.reshape(dec_batch, NSA_KV_HEADS, NSA_GROUP, dec_seq, NSA_KV_HEADS, HEAD_DIM)
    o = jnp.stack([o[:, g, :, :, g] for g in range(NSA_KV_HEADS)], axis=1)
    return o.transpose(0, 3, 1, 2, 4).reshape(dec_batch * dec_seq, NSA_WIDTH)


def _head_ones(width):
    h = np.arange(width) // HEAD_DIM
    return (h[:, None] == h[None, :]).astype(np.float32)


def _rwkv_pre_kernel(seq, use_carry, rw_ref, shift_ref, mu_ref, w0_ref, a0_ref, kk_ref, ka_ref, rk_ref,
                     w2_ref, a2_ref, g2_ref, ones_ref,
                     r_ref, w_ref, k_ref, kkn_ref, b_ref, v_ref, g_ref, bonus_ref, carry_ref):
    i = pl.program_id(0)
    x = rw_ref[...]
    tm = x.shape[0]
    row = lax.broadcasted_iota(jnp.int32, (tm, 1), 0)
    prev = pltpu.roll(x, 1, 0)
    if use_carry:
        @pl.when(i == 0)
        def _():
            carry_ref[...] = jnp.zeros(carry_ref.shape, F32)
        prev = jnp.where(row == 0, carry_ref[...], prev)
        carry_ref[...] = x[tm - 1:tm]
    prev = jnp.where(((i * tm + row) & (seq - 1)) == 0, shift_ref[...], prev)
    xm = x + (prev - x) * mu_ref[...]
    wd = RWKV_WIDTH
    r, k, v = xm[:, :wd], xm[:, wd:2 * wd], xm[:, 2 * wd:3 * wd]
    lwa = xm[:, 3 * wd:3 * wd + DECAY_LORA + ICLR_LORA]
    lg = xm[:, 3 * wd + DECAY_LORA + ICLR_LORA:]
    w = -jax.nn.softplus(-(w0_ref[...] + _dot_hi(jnp.tanh(lwa), w2_ref[...]))) - 0.5
    a = jax.nn.sigmoid(a0_ref[...] + _dot_hi(lwa, a2_ref[...]))
    ones = ones_ref[...]
    kk = k * kk_ref[...]
    kk = kk / jnp.maximum(jnp.sqrt(_dot_hi(kk * kk, ones)), 1e-12)
    k2 = k * (1.0 + (a - 1.0) * ka_ref[...])
    r_ref[...] = r
    w_ref[...] = jnp.exp(-jnp.exp(w))
    k_ref[...] = k2
    kkn_ref[...] = kk
    b_ref[...] = kk * a
    v_ref[...] = v
    g_ref[...] = _dot_hi(jax.nn.sigmoid(lg), g2_ref[...])
    bonus_ref[...] = _dot_hi(r * k2 * rk_ref[...], ones) * v


def _rwkv_pre(rw, shift, seq, tm, rwkv_w):
    mu, w0, w2, a0, a2, g2, k_k, k_a, r_k, _, _ = rwkv_w
    n = rw.shape[0]
    use_carry = seq >= tm
    row1 = lambda a: a.reshape(1, -1)
    z = jnp.zeros((DECAY_LORA, RWKV_WIDTH), F32)
    w2p = jnp.concatenate([w2, z], axis=0)
    a2p = jnp.concatenate([z, a2], axis=0)
    ones = jnp.asarray(_head_ones(RWKV_WIDTH))
    consts = [row1(mu), row1(w0), row1(a0), row1(k_k), row1(k_a), row1(r_k), w2p, a2p, g2, ones]
    full = lambda a: pl.BlockSpec(a.shape, lambda i: (0,) * a.ndim)
    if use_carry:
        shift_spec = pl.BlockSpec((None, 1, RWKV_PROJ), lambda i: (i // (seq // tm), 0, 0))
    else:
        shift_spec = pl.BlockSpec((tm, RWKV_PROJ), lambda i: (i, 0))
    tile = pl.BlockSpec((tm, RWKV_WIDTH), lambda i: (i, 0))
    return pl.pallas_call(
        functools.partial(_rwkv_pre_kernel, seq, use_carry),
        grid=(n // tm,),
        in_specs=[pl.BlockSpec((tm, RWKV_PROJ), lambda i: (i, 0)), shift_spec] + [full(c) for c in consts],
        out_specs=(tile,) * 8,
        out_shape=(jax.ShapeDtypeStruct((n, RWKV_WIDTH), F32),) * 8,
        scratch_shapes=[pltpu.VMEM((1, RWKV_PROJ), F32)],
        compiler_params=_cparams(("arbitrary",)),
        name="rwkv_pre",
    )(rw, shift, *consts)


SCAN_SEQS = 2
SCAN_UNROLL = 2


def _split3(x):
    hi = x.astype(BF16)
    r1 = x - hi.astype(F32)
    mid = r1.astype(BF16)
    lo = (r1 - mid.astype(F32)).astype(BF16)
    return hi, mid, lo


def _dot3_stacked(pieces, w):
    rows = pieces[0].shape[0]
    parts = [part for x in pieces for part in _split3(x)]
    out = _dot(jnp.concatenate(parts, axis=0), w)
    return [out[(3 * i) * rows:(3 * i + 1) * rows] + out[(3 * i + 1) * rows:(3 * i + 2) * rows]
            + out[(3 * i + 2) * rows:(3 * i + 3) * rows] for i in range(len(pieces))]


def _head_sums(ps, ones_half):
    half = ps[0].shape[1] // 2
    out = _dot3_stacked([p[:, :half] for p in ps] + [p[:, half:] for p in ps], ones_half)
    return [jnp.concatenate([out[i], out[len(ps) + i]], axis=1) for i in range(len(ps))]


def _scan_kernel(tc, r_ref, w_ref, k_ref, kk_ref, b_ref, vts_ref, s0_ref, ones_ref, wv_ref, wy_ref,
                 y_ref, s_ref, ypart_ref):
    tb = pl.program_id(1)
    n_blocks = vts_ref.shape[1]

    @pl.when(tb == 0)
    def _():
        s_ref[...] = s0_ref[...]

    ones_half = ones_ref[...]
    rows = (HEAD_DIM, RWKV_WIDTH)

    def block(blk, _):
        vts = [vts_ref[s, blk] for s in range(SCAN_SEQS)]

        def step(tp, _):
            t = blk * tc + tp
            bc = lambda ref, s: jnp.broadcast_to(ref[s, pl.ds(t, 1), :], rows)
            seqs = range(SCAN_SEQS)
            st = [s_ref[s] for s in seqs]
            sa = _head_sums([st[s] * bc(kk_ref, s) for s in seqs], ones_half)
            vcol = _dot3_stacked(vts, wv_ref[tp])
            st = [st[s] * bc(w_ref, s) - sa[s] * bc(b_ref, s) + vcol[s] * bc(k_ref, s) for s in seqs]
            y = _dot3_stacked([st[s] * bc(r_ref, s) for s in seqs], wy_ref[tp])
            for s in seqs:
                s_ref[s] = st[s]
                ypart_ref[s, tp] = y[s]
            return 0

        lax.fori_loop(0, tc, step, 0, unroll=SCAN_UNROLL)
        for s in range(SCAN_SEQS):
            acc = ypart_ref[s, 0]
            for tp in range(1, tc):
                acc = acc + ypart_ref[s, tp]
            y_ref[s, blk] = acc
        return 0

    lax.fori_loop(0, n_blocks, block, 0)


def _scan_consts(tc):
    wv = np.zeros((tc, LANES, RWKV_WIDTH), np.float32)
    for tp in range(tc):
        for h in range(RWKV_HEADS):
            wv[tp, tp * RWKV_HEADS + h, h * HEAD_DIM:(h + 1) * HEAD_DIM] = 1.0
    wy = np.transpose(wv, (0, 2, 1))
    return (jnp.asarray(_head_ones(RWKV_WIDTH // 2)).astype(BF16), jnp.asarray(wv).astype(BF16),
            jnp.asarray(wy).astype(BF16))


def _rwkv_scan(r, w, k, kk, b, v, s0, batch, seq, tt):
    tc = min(LANES // RWKV_HEADS, seq)
    nb = seq // tc
    to3 = lambda a: a.reshape(batch, seq, RWKV_WIDTH)
    vts = v.reshape(batch, nb, tc, RWKV_HEADS, HEAD_DIM).transpose(0, 1, 4, 2, 3).reshape(batch, nb, HEAD_DIM, tc * RWKV_HEADS)
    vts = jnp.pad(vts, ((0, 0), (0, 0), (0, 0), (0, LANES - tc * RWKV_HEADS)))
    s0l = s0.transpose(0, 2, 1, 3).reshape(batch, HEAD_DIM, RWKV_WIDTH)
    ones_half, wv, wy = _scan_consts(tc)
    full = lambda a: pl.BlockSpec(a.shape, lambda p, t: (0,) * a.ndim)
    seq_spec = pl.BlockSpec((SCAN_SEQS, tt, RWKV_WIDTH), lambda p, t: (p, t, 0))
    blk_spec = pl.BlockSpec((SCAN_SEQS, tt // tc, HEAD_DIM, LANES), lambda p, t: (p, t, 0, 0))
    st_spec = pl.BlockSpec((SCAN_SEQS, HEAD_DIM, RWKV_WIDTH), lambda p, t: (p, 0, 0))
    y4, s_out = pl.pallas_call(
        functools.partial(_scan_kernel, tc),
        grid=(batch // SCAN_SEQS, seq // tt),
        in_specs=[seq_spec] * 5 + [blk_spec, st_spec, full(ones_half), full(wv), full(wy)],
        out_specs=(blk_spec, st_spec),
        out_shape=(jax.ShapeDtypeStruct((batch, nb, HEAD_DIM, LANES), F32),
                   jax.ShapeDtypeStruct((batch, HEAD_DIM, RWKV_WIDTH), F32)),
        scratch_shapes=[pltpu.VMEM((SCAN_SEQS, tc, HEAD_DIM, LANES), F32)],
        compiler_params=_cparams(("arbitrary", "arbitrary")),
        name="rwkv_scan",
    )(to3(r), to3(w), to3(k), to3(kk), to3(b), vts, s0l, ones_half, wv, wy)
    y = y4[..., :tc * RWKV_HEADS].reshape(batch, nb, HEAD_DIM, tc, RWKV_HEADS).transpose(0, 1, 3, 4, 2)
    s_fin = s_out.reshape(batch, HEAD_DIM, RWKV_HEADS, HEAD_DIM).transpose(0, 2, 1, 3)
    return y.reshape(batch * seq, RWKV_WIDTH), s_fin


def _out_kernel(x_ref, nsa_ref, y_ref, bonus_ref, g_ref, gt_ref, lnw_ref, lnb_ref, ones_ref, wo_ref, gp_ref, o_ref):
    y = y_ref[...]
    ones = ones_ref[...]
    inv = 1.0 / HEAD_DIM
    yc = y - _dot_hi(y, ones) * inv
    var = _dot_hi(yc * yc, ones) * inv
    yn = yc * lax.rsqrt(var + GN_EPS) * lnw_ref[...] + lnb_ref[...]
    mr = ((yn + bonus_ref[...]) * g_ref[...]).astype(BF16)
    z = _dot(nsa_ref[...].astype(BF16), wo_ref[:NSA_WIDTH, :]) + _dot(mr, wo_ref[NSA_WIDTH:, :])
    zn = z * lax.rsqrt(jnp.mean(z * z, axis=-1, keepdims=True) + NORM_EPS) * gp_ref[...]
    o_ref[...] = x_ref[...] + gt_ref[...] * zn


def _mixer_out(x, nsa, y, bonus, g, mod, tokens_per_row, ln_w, ln_b, w_out, g_post, tm):
    n = x.shape[0]
    ones = jnp.asarray(_head_ones(RWKV_WIDTH))
    wo = w_out.astype(BF16)
    row1 = lambda a: a.reshape(1, -1)
    full = lambda a: pl.BlockSpec(a.shape, lambda i: (0,) * a.ndim)
    half = pl.BlockSpec((tm, RWKV_WIDTH), lambda i: (i, 0))
    xspec = pl.BlockSpec((tm, D_MODEL), lambda i: (i, 0))
    lw, lb, gp = row1(ln_w), row1(ln_b), row1(g_post)
    return pl.pallas_call(
        _out_kernel,
        grid=(n // tm,),
        in_specs=[xspec, half, half, half, half, _mod_specs(mod, tm, tokens_per_row, 2),
                  full(lw), full(lb), full(ones), full(wo), full(gp)],
        out_specs=xspec,
        out_shape=jax.ShapeDtypeStruct((n, D_MODEL), F32),
        compiler_params=_cparams(("arbitrary",)),
        name="mixer_out",
    )(x, nsa, y, bonus, g, mod, lw, lb, ones, wo, gp)


def _staircase():
    return [(a, b) for a in range(PEER_TOPK) for b in range(PEER_TOPK) if (a + 1) * (b + 1) <= PEER_TOPK]


def _top16_ranked(s):
    rows = s.shape[0]
    idx_row = lax.broadcasted_iota(jnp.int32, s.shape, 0)
    rank = jnp.full(s.shape, 99.0, F32)
    vals = []
    for it in range(PEER_TOPK):
        m = jnp.max(s, axis=0, keepdims=True)
        idx = jnp.min(jnp.where(s == m, idx_row, rows), axis=0, keepdims=True)
        hit = idx_row == idx
        rank = jnp.where(hit, np.float32(it), rank)
        s = jnp.where(hit, BELOW_NEG_INF, s)
        vals.append(m)
    return rank, vals


PEER_STRIP = 128
PEER_EXPERT_BLOCK = 1024
PEER_CHAIN = 256


def _peer_route(hT, wqT_ref, keys_ref, hd):
    n = hT.shape[1]
    stairs = _staircase()
    n_cand = -(-len(stairs) // SUBLANES) * SUBLANES
    cand_row = lax.broadcasted_iota(jnp.int32, (n_cand, n), 0)
    ranks, vals = [], []
    for c in range(2):
        q = _dot(wqT_ref[(hd * 2 + c) * PEER_HALF:(hd * 2 + c + 1) * PEER_HALF, :], hT)
        q = q * lax.rsqrt(jnp.mean(q * q, axis=0, keepdims=True) + NORM_EPS)
        s = _dot_hi(keys_ref[hd * 2 + c], q)
        rank, top = _top16_ranked(s)
        ranks.append((rank, s))
        vals.append(top)
    cand = jnp.concatenate([vals[0][a] + vals[1][b] for a, b in stairs]
                           + [jnp.full((n_cand - len(stairs), n), BELOW_NEG_INF, F32)], axis=0)
    chosen = jnp.zeros(cand.shape, F32)
    work = cand
    for _ in range(PEER_TOPK):
        m = jnp.max(work, axis=0, keepdims=True)
        idx = jnp.min(jnp.where(work == m, cand_row, n_cand), axis=0, keepdims=True)
        hit = cand_row == idx
        chosen = jnp.where(hit, 1.0, chosen)
        work = jnp.where(hit, BELOW_NEG_INF, work)
    top_sum = vals[0][0] + vals[1][0]
    z = jnp.sum(chosen * jnp.exp(jnp.where(chosen > 0.5, cand, top_sum) - top_sum), axis=0, keepdims=True)
    rank1, s1 = ranks[0]
    rank2, s2 = ranks[1]
    cnt = jnp.zeros(rank1.shape, F32)
    for a in range(PEER_TOPK):
        rows_a = [r for r, (ka, _) in enumerate(stairs) if ka == a]
        c_a = chosen[rows_a[0]:rows_a[0] + 1]
        for r in rows_a[1:]:
            c_a = c_a + chosen[r:r + 1]
        cnt = jnp.where(rank1 == np.float32(a), c_a, cnt)
    return jnp.exp(s1 - vals[0][0]) / z, cnt, jnp.exp(s2 - vals[1][0]), rank2


def _peer_kernel(x_ref, g2_ref, sc_ref, sh_ref, gt_ref, gp_ref, wqT_ref, keys_ref, u_ref, vT_ref,
                 o_ref, hT_ref, a_ref, cnt_ref, b_ref, r2_ref, acc_ref):
    i = pl.program_id(1)
    n = x_ref.shape[0]
    strip = min(PEER_STRIP, n)

    @pl.when(i == 0)
    def _():
        x = x_ref[...]
        h = x * lax.rsqrt(jnp.mean(x * x, axis=-1, keepdims=True) + NORM_EPS) * g2_ref[...]
        hT_ref[...] = (h * (1.0 + sc_ref[...]) + sh_ref[...]).T.astype(BF16)
        acc_ref[...] = jnp.zeros(acc_ref.shape, F32)
        for st in range(n // strip):
            lanes = slice(st * strip, (st + 1) * strip)
            hT = hT_ref[:, lanes]
            for hd in range(PEER_HEADS):
                a, cnt, bcol, r2 = _peer_route(hT, wqT_ref, keys_ref, hd)
                a_ref[hd, :, lanes] = a
                cnt_ref[hd, :, lanes] = cnt
                b_ref[hd, :, lanes] = bcol
                r2_ref[hd, :, lanes] = r2

    hT = hT_ref[...]
    per_chain = PEER_CHAIN // PEER_KEYS
    total = None
    for c in range(u_ref.shape[0] // PEER_CHAIN):
        experts = slice(c * PEER_CHAIN, (c + 1) * PEER_CHAIN)
        act = _gelu(_dot(u_ref[experts, :], hT))
        weighted = []
        for j in range(per_chain):
            blk = (i * (u_ref.shape[0] // PEER_KEYS)) + c * per_chain + j
            wgt = jnp.zeros((PEER_KEYS, n), F32)
            for hd in range(PEER_HEADS):
                a_row = a_ref[hd, pl.ds(blk, 1), :]
                c_row = cnt_ref[hd, pl.ds(blk, 1), :]
                wgt = wgt + jnp.where(r2_ref[hd] < c_row, b_ref[hd] * a_row, 0.0)
            weighted.append((wgt * act[j * PEER_KEYS:(j + 1) * PEER_KEYS]).astype(BF16))
        part = _dot(vT_ref[:, experts], jnp.concatenate(weighted, axis=0))
        total = part if total is None else total + part
    acc_ref[...] += total

    @pl.when(i == pl.num_programs(1) - 1)
    def _():
        z = acc_ref[...].T
        zn = z * lax.rsqrt(jnp.mean(z * z, axis=-1, keepdims=True) + NORM_EPS) * gp_ref[...]
        o_ref[...] = x_ref[...] + gt_ref[...] * zn


def _peer_layer(x, g_pre, mod, tokens_per_row, g_post, w_q, sub_keys, u_bf, vT_bf, tm):
    n = x.shape[0]
    wqT = w_q.astype(BF16).T
    keys = sub_keys.reshape(PEER_HEADS * 2, PEER_KEYS, PEER_HALF)
    eb = PEER_EXPERT_BLOCK
    row1 = lambda a: a.reshape(1, -1)
    g2, gp = row1(g_pre), row1(g_post)
    full = lambda a: pl.BlockSpec(a.shape, lambda t, i: (0,) * a.ndim)
    modspec = lambda col: _mod_specs(mod, tm, tokens_per_row, col)
    xspec = pl.BlockSpec((tm, D_MODEL), lambda t, i: (t, 0))
    per_head = pltpu.VMEM((PEER_HEADS, PEER_KEYS, tm), F32)
    return pl.pallas_call(
        _peer_kernel,
        grid=(n // tm, u_bf.shape[0] // eb),
        in_specs=[xspec, full(g2), modspec(4), modspec(3), modspec(5), full(gp), full(wqT), full(keys),
                  pl.BlockSpec((eb, D_MODEL), lambda t, i: (i, 0)),
                  pl.BlockSpec((D_MODEL, eb), lambda t, i: (0, i))],
        out_specs=xspec,
        out_shape=jax.ShapeDtypeStruct((n, D_MODEL), F32),
        scratch_shapes=[pltpu.VMEM((D_MODEL, tm), BF16), per_head, per_head, per_head, per_head,
                        pltpu.VMEM((D_MODEL, tm), F32)],
        compiler_params=_cparams(("arbitrary", "arbitrary")),
        name="peer",
    )(x, g2, mod, mod, mod, gp, wqT, keys, u_bf, vT_bf)


def kernel(x_prompt, x_sample, c_prompt, c_sample, cache_nsa, page_table, cache_win, state_wkv, state_shift, w_ada, b_ada, norm_pre1, norm_post1, norm_pre2, norm_post2, w_in, w_out, cmp_k_w1, cmp_k_w2, cmp_k_pe, cmp_v_w1, cmp_v_w2, cmp_v_pe, rwkv_mu, rwkv_w0, rwkv_w2, rwkv_a0, rwkv_a2, rwkv_g2, rwkv_k_k, rwkv_k_a, rwkv_r_k, rwkv_ln_w, rwkv_ln_b, peer_w_q, peer_sub_keys, peer_u, peer_v):
    batch, seq, d = x_prompt.shape
    dec_batch, dec_seq, _ = x_sample.shape
    depth = w_ada.shape[0]
    assert d == D_MODEL and seq % WINDOW == 0 and seq & (seq - 1) == 0 and dec_seq & (dec_seq - 1) == 0
    assert batch % SCAN_SEQS == 0 and dec_batch % SCAN_SEQS == 0
    n_p, n_s = batch * seq, dec_batch * dec_seq
    tile = lambda n: min(256, n)
    xp = x_prompt.reshape(n_p, d)
    xs = x_sample.reshape(n_s, d)
    c_all = jnp.concatenate([c_prompt, c_sample], axis=0)
    c_all = jnp.pad(c_all, ((0, -c_all.shape[0] % SUBLANES), (0, 0)))
    w_keep = min(WINDOW, seq)
    outs = [[] for _ in range(8)]
    for l in range(depth):
        mod = _modulation(c_all, w_ada[l], b_ada[l])
        mod_p = mod[:batch].reshape(batch, 1, 6 * d)
        mod_s = jnp.repeat(mod[batch:batch + dec_batch], dec_seq, axis=0)
        kw = _compress_weights(cmp_k_w1[l], cmp_k_w2[l], cmp_k_pe[l])
        vw = _compress_weights(cmp_v_w1[l], cmp_v_w2[l], cmp_v_pe[l])
        rwkv_w = (rwkv_mu[l], rwkv_w0[l], rwkv_w2[l], rwkv_a0[l], rwkv_a2[l], rwkv_g2[l],
                  rwkv_k_k[l], rwkv_k_a[l], rwkv_r_k[l], rwkv_ln_w[l], rwkv_ln_b[l])
        u_bf = peer_u[l].astype(BF16)
        vT_bf = peer_v[l].astype(BF16).T

        qT, kv, kvb, kvT, gT, rw = _in_projection(xp, norm_pre1[l], mod_p, seq, w_in[l], tile(n_p))
        kc, vcT, kbias, vbias = _compress_prompt(kv, batch, seq, kw, vw)
        nsa = _nsa_prompt(qT, kvb, kvT, gT, kc, vcT, batch, seq)
        r, w, k, kk, b, v, g, bonus = _rwkv_pre(rw, jnp.zeros((batch, 1, RWKV_PROJ), F32), seq, tile(n_p), rwkv_w)
        y, wkv_p = _rwkv_scan(r, w, k, kk, b, v, jnp.zeros((batch, RWKV_HEADS, HEAD_DIM, HEAD_DIM), F32),
                              batch, seq, min(256, seq))
        x1 = _mixer_out(xp, nsa, y, bonus, g, mod_p, seq, rwkv_ln_w[l], rwkv_ln_b[l], w_out[l], norm_post1[l], tile(n_p))
        xp = _peer_layer(x1, norm_pre2[l], mod_p, seq, norm_post2[l], peer_w_q[l], peer_sub_keys[l], u_bf, vT_bf, tile(n_p))
        kv3 = kv.reshape(batch, seq, KV6)
        outs[0].append(kv3[:, :, :4 * KV_WIDTH].reshape(batch, seq, 4, NSA_KV_HEADS, HEAD_DIM))
        outs[2].append(kv3[:, seq - w_keep:, 4 * KV_WIDTH:].reshape(batch, w_keep, 2, NSA_KV_HEADS, HEAD_DIM))
        outs[4].append(wkv_p)
        outs[6].append(rw.reshape(batch, seq, RWKV_PROJ)[:, -1])

        qT, kv, _, _, gT, rw = _in_projection(xs, norm_pre1[l], mod_s, None, w_in[l], tile(n_s))
        cacheT = jnp.transpose(cache_nsa[l], (0, 2, 3, 4, 1)).reshape(cache_nsa.shape[1], CACHE_CHANNELS, cache_nsa.shape[2])
        winT = jnp.transpose(cache_win[l], (0, 2, 3, 4, 1)).reshape(dec_batch, 2 * KV_WIDTH, cache_win.shape[2])
        nsa = _nsa_sample(qT.T, gT, kv, cacheT, page_table, winT, kw, vw, kbias, vbias, dec_batch, dec_seq)
        shift_tok = jnp.repeat(state_shift[l], dec_seq, axis=0)
        r, w, k, kk, b, v, g, bonus = _rwkv_pre(rw, shift_tok, dec_seq, tile(n_s), rwkv_w)
        y, wkv_s = _rwkv_scan(r, w, k, kk, b, v, state_wkv[l], dec_batch, dec_seq, dec_seq)
        x1 = _mixer_out(xs, nsa, y, bonus, g, mod_s, None, rwkv_ln_w[l], rwkv_ln_b[l], w_out[l], norm_post1[l], tile(n_s))
        xs = _peer_layer(x1, norm_pre2[l], mod_s, None, norm_post2[l], peer_w_q[l], peer_sub_keys[l], u_bf, vT_bf, tile(n_s))
        kv3 = kv.reshape(dec_batch, dec_seq, KV6)
        outs[1].append(kv3[:, :, :4 * KV_WIDTH].reshape(dec_batch, dec_seq, 4, NSA_KV_HEADS, HEAD_DIM))
        new_win = kv3[:, :, 4 * KV_WIDTH:].reshape(dec_batch, dec_seq, 2, NSA_KV_HEADS, HEAD_DIM)
        outs[3].append(jnp.concatenate([cache_win[l], new_win], axis=1)[:, dec_seq:])
        outs[5].append(wkv_s)
        outs[7].append(rw.reshape(dec_batch, dec_seq, RWKV_PROJ)[:, -1])
    return (xp.reshape(batch, seq, d), xs.reshape(dec_batch, dec_seq, d)) + tuple(jnp.stack(o) for o in outs)
```

```python
import functools

import numpy as np
import jax
import jax.numpy as jnp
from jax import lax
from jax.experimental import pallas as pl
from jax.experimental.pallas import tpu as pltpu

F32 = jnp.float32
BF16 = jnp.bfloat16
HIGHEST = lax.Precision.HIGHEST

LANES = 128
SUBLANES = 8
VMEM_LIMIT_BYTES = 56 * 1024 * 1024

D_MODEL = 1024
HEAD_DIM = 64
NORM_EPS = 1e-6
NEG_INF = -1e30
BELOW_NEG_INF = -3e38
NSA_HEADS = 8
NSA_KV_HEADS = 2
NSA_GROUP = NSA_HEADS // NSA_KV_HEADS
NSA_WIDTH = NSA_HEADS * HEAD_DIM
KV_WIDTH = NSA_KV_HEADS * HEAD_DIM
CMP_STRIDE = 16
CMP_LEN = 32
CMP_HIDDEN = 2 * HEAD_DIM
SEL_BLOCK = 64
SEL_TOPK = 16
WINDOW = 512
Q_BLOCK = 128
FORCE_BONUS = 1e4
ATTN_SCALE = HEAD_DIM ** -0.5
RWKV_HEADS = 8
RWKV_WIDTH = RWKV_HEADS * HEAD_DIM
DECAY_LORA = 64
ICLR_LORA = 64
GATE_LORA = 128
GN_EPS = 64e-5
NSA_PROJ = NSA_WIDTH + 6 * KV_WIDTH + 3 * NSA_HEADS
RWKV_PROJ = 3 * RWKV_WIDTH + DECAY_LORA + ICLR_LORA + GATE_LORA
KV6 = 6 * KV_WIDTH
GATE_ROWS = 32
PEER_HEADS = 8
PEER_KEYS = 128
PEER_HALF = 128
PEER_TOPK = 16
KEY_CHUNK = 128
SEL_CHUNK = 256


def _cparams(sem):
    return pltpu.CompilerParams(dimension_semantics=sem, vmem_limit_bytes=VMEM_LIMIT_BYTES)


def _gelu(x):
    return 0.5 * x * (1.0 + lax.erf(x * np.float32(0.7071067811865476)))


def _dot(a, b):
    return jnp.dot(a, b, preferred_element_type=F32)


def _dot_nt(a, b):
    return lax.dot_general(a, b, (((1,), (1,)), ((), ())), preferred_element_type=F32)


def _dot_hi(a, b):
    return jnp.dot(a, b, preferred_element_type=F32, precision=HIGHEST)


def _mod_kernel(c_ref, w_ref, b_ref, o_ref):
    c = c_ref[...]
    o_ref[...] = _dot_hi(c * jax.nn.sigmoid(c), w_ref[...]) + b_ref[...]


def _modulation(c, w_ada, b_ada):
    rows, d = c.shape
    n = w_ada.shape[1]
    tn = 1536
    return pl.pallas_call(
        _mod_kernel,
        grid=(n // tn,),
        in_specs=[pl.BlockSpec((rows, d), lambda j: (0, 0)),
                  pl.BlockSpec((d, tn), lambda j: (0, j)),
                  pl.BlockSpec((1, tn), lambda j: (0, j))],
        out_specs=pl.BlockSpec((rows, tn), lambda j: (0, j)),
        out_shape=jax.ShapeDtypeStruct((rows, n), F32),
        compiler_params=_cparams(("arbitrary",)),
        name="modulation",
    )(c, w_ada, b_ada.reshape(1, n))


def _inproj_kernel(x_ref, g_ref, sc_ref, sh_ref, wqT_ref, wkv_ref, wkvT_ref, wgT_ref, wr_ref,
                   qT_ref, kv_ref, kvb_ref, kvT_ref, gT_ref, rw_ref):
    x = x_ref[...]
    h = x * lax.rsqrt(jnp.mean(x * x, axis=-1, keepdims=True) + NORM_EPS) * g_ref[...]
    h = (h * (1.0 + sc_ref[...]) + sh_ref[...]).astype(BF16)
    qT_ref[...] = _dot_nt(wqT_ref[...], h)
    kv = _dot(h, wkv_ref[...])
    kv_ref[...] = kv
    kvb_ref[...] = kv.astype(BF16)
    kvT_ref[...] = _dot_nt(wkvT_ref[...], h).astype(BF16)
    gT_ref[...] = jax.nn.sigmoid(_dot_nt(wgT_ref[...], h))
    rw_ref[...] = _dot(h, wr_ref[...])


def _mod_specs(mod, tm, tokens_per_row, col):
    if tokens_per_row is None:
        return pl.BlockSpec((tm, D_MODEL), lambda i, *_: (i, col))
    tiles_per_row = tokens_per_row // tm
    return pl.BlockSpec((None, 1, D_MODEL), lambda i, *_: (i // tiles_per_row, 0, col))


def _in_projection(x, g_pre, mod, tokens_per_row, w_in, tm):
    n = x.shape[0]
    wb = w_in.astype(BF16)
    wq = wb[:, :NSA_WIDTH]
    wkv = wb[:, NSA_WIDTH:NSA_WIDTH + KV6]
    wg = jnp.pad(wb[:, NSA_WIDTH + KV6:NSA_PROJ], ((0, 0), (0, GATE_ROWS - 3 * NSA_HEADS)))
    wr = wb[:, NSA_PROJ:]
    full = lambda a: pl.BlockSpec(a.shape, lambda i: (0,) * a.ndim)
    wqT, wkvT, wgT = wq.T, wkv.T, wg.T
    g2 = g_pre.reshape(1, D_MODEL)
    outs = (
        jax.ShapeDtypeStruct((NSA_WIDTH, n), F32),
        jax.ShapeDtypeStruct((n, KV6), F32),
        jax.ShapeDtypeStruct((n, KV6), BF16),
        jax.ShapeDtypeStruct((KV6, n), BF16),
        jax.ShapeDtypeStruct((GATE_ROWS, n), F32),
        jax.ShapeDtypeStruct((n, RWKV_PROJ), F32),
    )
    return pl.pallas_call(
        _inproj_kernel,
        grid=(n // tm,),
        in_specs=[pl.BlockSpec((tm, D_MODEL), lambda i: (i, 0)), full(g2),
                  _mod_specs(mod, tm, tokens_per_row, 1), _mod_specs(mod, tm, tokens_per_row, 0),
                  full(wqT), full(wkv), full(wkvT), full(wgT), full(wr)],
        out_specs=(pl.BlockSpec((NSA_WIDTH, tm), lambda i: (0, i)),
                   pl.BlockSpec((tm, KV6), lambda i: (i, 0)),
                   pl.BlockSpec((tm, KV6), lambda i: (i, 0)),
                   pl.BlockSpec((KV6, tm), lambda i: (0, i)),
                   pl.BlockSpec((GATE_ROWS, tm), lambda i: (0, i)),
                   pl.BlockSpec((tm, RWKV_PROJ), lambda i: (i, 0))),
        out_shape=outs,
        compiler_params=_cparams(("arbitrary",)),
        name="in_projection",
    )(x, g2, mod, mod, wqT, wkv, wkvT, wgT, wr)


def _compress_weights(w1, w2, pe):
    z = jnp.zeros((CMP_STRIDE, HEAD_DIM, CMP_HIDDEN), F32)
    a, b = w1[:CMP_STRIDE], w1[CMP_STRIDE:]
    top = jnp.concatenate([a, z, b, z], axis=-1)
    bot = jnp.concatenate([z, a, z, b], axis=-1)
    wbd = jnp.concatenate([top, bot], axis=1)
    half = CMP_STRIDE // 2
    wbd = jnp.concatenate([wbd[:half], wbd[half:]], axis=1).astype(BF16)
    z2 = jnp.zeros((CMP_HIDDEN, HEAD_DIM), F32)
    w2bd = jnp.concatenate([jnp.concatenate([w2, z2], axis=1),
                            jnp.concatenate([z2, w2], axis=1)], axis=0).astype(BF16)
    return wbd, w2bd, w2bd.T, pe.reshape(1, CMP_LEN * HEAD_DIM), w1.reshape(CMP_LEN * HEAD_DIM, CMP_HIDDEN)


def _compress_bias(pe_ref, w1_ref):
    b = _dot_hi(jnp.broadcast_to(pe_ref[...], (SUBLANES, CMP_LEN * HEAD_DIM)), w1_ref[...])[0:1]
    return jnp.concatenate([b, b], axis=1)


def _compress_hidden(rows_at, n_chunks, wbd_ref, bias):
    acc = jnp.zeros((n_chunks, 4 * CMP_HIDDEN), F32)
    half = CMP_STRIDE // 2
    for l in range(half):
        x = jnp.concatenate([rows_at(l), rows_at(l + half)], axis=1).astype(BF16)
        acc = acc + _dot(x, wbd_ref[l])
    first, second = acc[:, :2 * CMP_HIDDEN], acc[:, 2 * CMP_HIDDEN:]
    hid = first + pltpu.roll(second, n_chunks - 1, 0) + bias
    return _gelu(hid).astype(BF16)


def _cmp_prompt_kernel(k_ref, v_ref, kwbd_ref, kw2_ref, kpe_ref, kw1_ref, vwbd_ref, vw2T_ref, vpe_ref, vw1_ref,
                       kc_ref, vcT_ref, kb_ref, vb_ref):
    n_chunks = k_ref.shape[0] // CMP_STRIDE
    kb = _compress_bias(kpe_ref, kw1_ref)
    vb = _compress_bias(vpe_ref, vw1_ref)
    kb_ref[...] = kb
    vb_ref[...] = vb
    gk = _compress_hidden(lambda l: k_ref[pl.ds(l, n_chunks, stride=CMP_STRIDE), :], n_chunks, kwbd_ref, kb)
    kc_ref[...] = _dot(gk, kw2_ref[...]).astype(BF16)
    gv = _compress_hidden(lambda l: v_ref[pl.ds(l, n_chunks, stride=CMP_STRIDE), :], n_chunks, vwbd_ref, vb)
    vcT_ref[...] = _dot_nt(vw2T_ref[...], gv).astype(BF16)


def _compress_prompt(kv, batch, seq, kw, vw):
    nc = seq // CMP_STRIDE
    kwbd, kw2, _, kpe, kw1 = kw
    vwbd, _, vw2T, vpe, vw1 = vw
    full = lambda a: pl.BlockSpec(a.shape, lambda b: (0,) * a.ndim)
    return pl.pallas_call(
        _cmp_prompt_kernel,
        grid=(batch,),
        in_specs=[pl.BlockSpec((seq, KV_WIDTH), lambda b: (b, 0)),
                  pl.BlockSpec((seq, KV_WIDTH), lambda b: (b, 1)),
                  full(kwbd), full(kw2), full(kpe), full(kw1), full(vwbd), full(vw2T), full(vpe), full(vw1)],
        out_specs=(pl.BlockSpec((None, nc, KV_WIDTH), lambda b: (b, 0, 0)),
                   pl.BlockSpec((None, KV_WIDTH, nc), lambda b: (b, 0, 0)),
                   pl.BlockSpec((1, 2 * CMP_HIDDEN), lambda b: (0, 0)),
                   pl.BlockSpec((1, 2 * CMP_HIDDEN), lambda b: (0, 0))),
        out_shape=(jax.ShapeDtypeStruct((batch, nc, KV_WIDTH), BF16),
                   jax.ShapeDtypeStruct((batch, KV_WIDTH, nc), BF16),
                   jax.ShapeDtypeStruct((1, 2 * CMP_HIDDEN), F32),
                   jax.ShapeDtypeStruct((1, 2 * CMP_HIDDEN), F32)),
        compiler_params=_cparams(("arbitrary",)),
        name="compress_prompt",
    )(kv, kv, kwbd, kw2, kpe, kw1, vwbd, vw2T, vpe, vw1)


def _overlap_T(n_sel_rows, n_cmp_cols, n_sel, n_cmp):
    s = np.arange(n_sel_rows)[:, None] * SEL_BLOCK
    c = np.arange(n_cmp_cols)[None, :] * CMP_STRIDE
    ov = (c < s + SEL_BLOCK) & (c + CMP_LEN > s)
    ov &= (np.arange(n_sel_rows)[:, None] < n_sel) & (np.arange(n_cmp_cols)[None, :] < n_cmp)
    return jnp.asarray(ov.astype(np.float32))


def _masked_softmax_cols(s, mask):
    s = jnp.where(mask, s, NEG_INF)
    m = jnp.max(s, axis=0, keepdims=True)
    e = jnp.where(mask, jnp.exp(s - m), 0.0)
    return e / jnp.maximum(jnp.sum(e, axis=0, keepdims=True), 1e-30)


def _select_blocks(imp, q_pos, n_sel):
    rows = imp.shape[0]
    blk = lax.broadcasted_iota(jnp.int32, imp.shape, 0)
    cur = jnp.right_shift(q_pos, 6)
    valid = blk <= cur
    forced = (blk == 0) | (blk == cur) | (blk == cur - 1)
    score = jnp.where(valid, imp + jnp.where(forced, FORCE_BONUS, 0.0), NEG_INF)
    score = jnp.where(blk < n_sel, score, BELOW_NEG_INF)
    sel = jnp.zeros(imp.shape, F32)
    for _ in range(min(SEL_TOPK, n_sel)):
        m = jnp.max(score, axis=0, keepdims=True)
        idx = jnp.min(jnp.where(score == m, blk, rows), axis=0, keepdims=True)
        hit = blk == idx
        sel = jnp.where(hit & (m > 0.5 * NEG_INF), 1.0, sel)
        score = jnp.where(hit, BELOW_NEG_INF, score)
    return sel


def _online_update(carry, s, mask, weighted_values):
    m, l, acc = carry
    s = jnp.where(mask, s, NEG_INF)
    m_new = jnp.maximum(m, jnp.max(s, axis=0, keepdims=True))
    alpha = jnp.exp(m - m_new)
    e = jnp.where(mask, jnp.exp(s - m_new), 0.0)
    l = alpha * l + jnp.sum(e, axis=0, keepdims=True)
    acc = alpha * acc + weighted_values(e.astype(BF16))
    return m_new, l, acc


def _online_update_biased(carry, s, weighted_values):
    m, l, acc = carry
    m_new = jnp.maximum(m, jnp.max(s, axis=0, keepdims=True))
    alpha = jnp.exp(m - m_new)
    e = jnp.exp(s - m_new)
    l = alpha * l + jnp.sum(e, axis=0, keepdims=True)
    acc = alpha * acc + weighted_values(e.astype(BF16))
    return m_new, l, acc


def _online_init(rows, cols):
    return (jnp.full((1, cols), NEG_INF, F32), jnp.zeros((1, cols), F32), jnp.zeros((rows, cols), F32))


def _online_finish(carry):
    _, l, acc = carry
    return acc / jnp.maximum(l, 1e-30)


def _nsa_prompt_kernel(qT_ref, ksel_ref, kwin_ref, vselT_ref, vwinT_ref, kc_ref, vcT_ref, gT_ref, ovT_ref,
                       o_ref, sel_ref):
    qb = pl.program_id(1)
    n_cmp_rows = kc_ref.shape[0]
    n_sel = ovT_ref.shape[0]
    cols = NSA_GROUP * Q_BLOCK
    pos0 = qb * Q_BLOCK
    groups = range(NSA_KV_HEADS)
    qT = qT_ref[...] * ATTN_SCALE
    gates = gT_ref[...]
    q_pos1 = pos0 + lax.broadcasted_iota(jnp.int32, (1, Q_BLOCK), 1)
    q_pos = jnp.concatenate([q_pos1] * NSA_GROUP, axis=1)
    key_row = lax.broadcasted_iota(jnp.int32, (KEY_CHUNK, cols), 0)
    sel_row = lax.broadcasted_iota(jnp.int32, (SEL_CHUNK, cols), 0)
    zeros_half = jnp.zeros((HEAD_DIM, cols), F32)
    rows_of = lambda g: slice(g * HEAD_DIM, (g + 1) * HEAD_DIM)

    qbd, o_c = [], []
    for g in groups:
        qg = jnp.concatenate([qT[(g * NSA_GROUP + r) * HEAD_DIM:(g * NSA_GROUP + r + 1) * HEAD_DIM, :]
                              for r in range(NSA_GROUP)], axis=1)
        qbd.append(jnp.concatenate([qg, zeros_half] if g == 0 else [zeros_half, qg], axis=0).astype(BF16))
        s = _dot(kc_ref[...], qbd[g])
        n_idx = lax.broadcasted_iota(jnp.int32, s.shape, 0)
        cmask = (n_idx * CMP_STRIDE + (CMP_LEN - 1) <= q_pos) & (n_idx < n_cmp_rows - 1)
        p = _masked_softmax_cols(s, cmask)
        o_c.append(_dot(vcT_ref[rows_of(g), :], p.astype(BF16)))
        p_sum = p[:, 0:Q_BLOCK]
        for r in range(1, NSA_GROUP):
            p_sum = p_sum + p[:, r * Q_BLOCK:(r + 1) * Q_BLOCK]
        imp = _dot_hi(ovT_ref[...], p_sum)
        sel = _select_blocks(imp, q_pos1, n_sel)
        sel_ref[g] = jnp.concatenate([jnp.where(sel > 0.5, 0.0, NEG_INF)] * NSA_GROUP, axis=1)

    def sel_scores(j, g):
        keys = pl.ds(pl.multiple_of(j * SEL_CHUNK, SEL_CHUNK), SEL_CHUNK)
        blocks_per_chunk = SEL_CHUNK // SEL_BLOCK
        bias = jnp.concatenate(
            [jnp.broadcast_to(sel_ref[g, pl.ds(j * blocks_per_chunk + i, 1), :], (SEL_BLOCK, cols))
             for i in range(blocks_per_chunk)], axis=0)
        return _dot(ksel_ref[keys, :], qbd[g]) + bias, functools.partial(_dot, vselT_ref[rows_of(g), keys])

    def sel_step(j, carry):
        return tuple(_online_update_biased(carry[g], *sel_scores(j, g)) for g in groups)
    n_full = pos0 // SEL_CHUNK
    carry = lax.fori_loop(0, n_full, sel_step, tuple(_online_init(HEAD_DIM, cols) for _ in groups))
    causal = n_full * SEL_CHUNK + sel_row <= q_pos
    o_s = []
    for g in groups:
        sc, weighted_values = sel_scores(n_full, g)
        o_s.append(_online_finish(_online_update_biased(carry[g], jnp.where(causal, sc, NEG_INF), weighted_values)))

    def win_step(i, carry):
        jj = qb - WINDOW // KEY_CHUNK + i
        keys = pl.ds(pl.multiple_of(jnp.maximum(jj, 0) * KEY_CHUNK, KEY_CHUNK), KEY_CHUNK)
        k = kwin_ref[keys, :]
        k_pos = jj * KEY_CHUNK + key_row
        mask = (k_pos <= q_pos) & (q_pos - k_pos < WINDOW) & (k_pos >= 0)
        return tuple(_online_update(carry[g], _dot(k, qbd[g]), mask, functools.partial(_dot, vwinT_ref[rows_of(g), keys]))
                     for g in groups)
    carry = lax.fori_loop(0, WINDOW // KEY_CHUNK + 1, win_step, tuple(_online_init(HEAD_DIM, cols) for _ in groups))
    o_w = [_online_finish(carry[g]) for g in groups]

    out_rows = []
    for g in groups:
        for r in range(NSA_GROUP):
            h = g * NSA_GROUP + r
            cs = slice(r * Q_BLOCK, (r + 1) * Q_BLOCK)
            out_rows.append(gates[3 * h:3 * h + 1] * o_c[g][:, cs] + gates[3 * h + 1:3 * h + 2] * o_s[g][:, cs]
                            + gates[3 * h + 2:3 * h + 3] * o_w[g][:, cs])
    o_ref[...] = jnp.concatenate(out_rows, axis=0).T


def _nsa_prompt(qT, kvb, kvT, gT, kc, vcT, batch, seq):
    nq = seq // Q_BLOCK
    nc = seq // CMP_STRIDE
    n_sel = seq // SEL_BLOCK
    ovT = _overlap_T(n_sel, nc, n_sel, nc - 1)
    cols = NSA_GROUP * Q_BLOCK
    return pl.pallas_call(
        _nsa_prompt_kernel,
        grid=(batch, nq),
        in_specs=[pl.BlockSpec((NSA_WIDTH, Q_BLOCK), lambda b, q: (0, b * nq + q)),
                  pl.BlockSpec((seq, KV_WIDTH), lambda b, q: (b, 2)),
                  pl.BlockSpec((seq, KV_WIDTH), lambda b, q: (b, 4)),
                  pl.BlockSpec((KV_WIDTH, seq), lambda b, q: (3, b)),
                  pl.BlockSpec((KV_WIDTH, seq), lambda b, q: (5, b)),
                  pl.BlockSpec((None, nc, KV_WIDTH), lambda b, q: (b, 0, 0)),
                  pl.BlockSpec((None, KV_WIDTH, nc), lambda b, q: (b, 0, 0)),
                  pl.BlockSpec((GATE_ROWS, Q_BLOCK), lambda b, q: (0, b * nq + q)),
                  pl.BlockSpec((n_sel, nc), lambda b, q: (0, 0))],
        out_specs=pl.BlockSpec((Q_BLOCK, NSA_WIDTH), lambda b, q: (b * nq + q, 0)),
        out_shape=jax.ShapeDtypeStruct((batch * seq, NSA_WIDTH), F32),
        scratch_shapes=[pltpu.VMEM((NSA_KV_HEADS, n_sel, cols), F32)],
        compiler_params=_cparams(("arbitrary", "arbitrary")),
        name="nsa_prompt",
    )(qT, kvb, kvb, kvT, kvT, kc, vcT, gT, ovT)


CACHE_PARTS = 4
CACHE_CHANNELS = CACHE_PARTS * KV_WIDTH
TRANSPOSE_CHUNK = 512


def _softmax_two(s_a, mask_a, s_b, mask_b):
    m = jnp.maximum(jnp.max(jnp.where(mask_a, s_a, NEG_INF), axis=-1, keepdims=True),
                    jnp.max(jnp.where(mask_b, s_b, NEG_INF), axis=-1, keepdims=True))
    e_a = jnp.where(mask_a, jnp.exp(s_a - m), 0.0)
    e_b = jnp.where(mask_b, jnp.exp(s_b - m), 0.0)
    total = jnp.sum(e_a, axis=-1, keepdims=True) + jnp.sum(e_b, axis=-1, keepdims=True)
    return e_a.astype(BF16), e_b.astype(BF16), 1.0 / jnp.maximum(total, 1e-30)


def _nsa_sample_kernel(dec_seq, n_sel, pt_ref, q_ref, gate_ref, newT_ref, winT_ref, cache_ref, expand_hbm,
                       kwbd_ref, kw2_ref, kb_ref, vwbd_ref, vw2_ref, vb_ref, ovT_ref, fold_ref,
                       o_ref, buf_ref, rows_ref, expand_ref, sem_ref, esem_ref):
    b = pl.program_id(0)
    nb = pl.num_programs(0)
    n_pages = pt_ref.shape[1]
    page = cache_ref.shape[2]
    p_len = n_pages * page
    n_chunks = p_len // CMP_STRIDE
    n_rows = o_ref.shape[0]
    slot = b % 2

    def page_copy(seq, i, sl):
        return pltpu.make_async_copy(cache_ref.at[pt_ref[seq, i]],
                                     buf_ref.at[sl, :, pl.ds(i * page, page)], sem_ref.at[sl])

    def start_fetch(seq, sl):
        def body(i, _):
            page_copy(seq, i, sl).start()
            return 0
        lax.fori_loop(0, n_pages, body, 0)

    expand_copy = pltpu.make_async_copy(expand_hbm, expand_ref, esem_ref.at[0])

    @pl.when(b == 0)
    def _():
        expand_copy.start()
        start_fetch(0, 0)

    @pl.when(b + 1 < nb)
    def _():
        start_fetch(b + 1, 1 - slot)

    def wait_body(i, _):
        page_copy(b, i, slot).wait()
        return 0
    lax.fori_loop(0, n_pages, wait_body, 0)

    @pl.when(b == 0)
    def _():
        expand_copy.wait()

    q_all = (q_ref[...] * ATTN_SCALE).astype(BF16)
    col = lax.broadcasted_iota(jnp.int32, (1, LANES), 1)
    q_pos = p_len + (col & (dec_seq - 1))

    def compressed(part, wbd_ref, bias_ref):
        def move(j, _):
            cols = pl.ds(pl.multiple_of(j * TRANSPOSE_CHUNK, TRANSPOSE_CHUNK), TRANSPOSE_CHUNK)
            rows_ref[cols, :] = buf_ref[slot, part * KV_WIDTH:(part + 1) * KV_WIDTH, cols].T
            return 0
        lax.fori_loop(0, p_len // TRANSPOSE_CHUNK, move, 0)
        return _compress_hidden(lambda l: rows_ref[pl.ds(l, n_chunks, stride=CMP_STRIDE), :], n_chunks,
                                wbd_ref, bias_ref[...])

    kc = _dot(compressed(0, kwbd_ref, kb_ref), kw2_ref[...]).astype(BF16)
    vc = _dot(compressed(1, vwbd_ref, vb_ref), vw2_ref[...]).astype(BF16)

    s = _dot_nt(kc, q_all)
    n_idx = lax.broadcasted_iota(jnp.int32, s.shape, 0)
    cmask = (n_idx * CMP_STRIDE + (CMP_LEN - 1) <= q_pos) & (n_idx < n_chunks - 1)
    p = _masked_softmax_cols(s, cmask)
    o_c = lax.dot_general(p.astype(BF16), vc, (((0,), (0,)), ((), ())), preferred_element_type=F32)
    imp = _dot_hi(_dot_hi(ovT_ref[...], p), fold_ref[...])
    sel = _select_blocks(imp, q_pos, n_sel)
    selT = _dot_hi(sel, fold_ref[...].T).T[:n_rows]

    q_rows = q_all[:n_rows]
    row_tok = lax.broadcasted_iota(jnp.int32, (n_rows, 1), 0) & (dec_seq - 1)
    lane = lax.broadcasted_iota(jnp.int32, (n_rows, LANES), 1)
    new_ok = (lane <= row_tok) & (lane < dec_seq)
    newT = newT_ref[...].astype(BF16)
    part = lambda a, i: a[i * KV_WIDTH:(i + 1) * KV_WIDTH]

    chosen = _dot(selT.astype(BF16), expand_ref[...]) > 0.5
    s_past = _dot(q_rows, buf_ref[slot, 2 * KV_WIDTH:3 * KV_WIDTH, :].astype(BF16))
    new_blk = p_len // SEL_BLOCK
    new_sel = (selT[:, new_blk:new_blk + 1] > 0.5) & new_ok
    e_p, e_n, inv = _softmax_two(s_past, chosen, _dot(q_rows, part(newT, 2)), new_sel)
    o_s = (_dot_nt(e_p, buf_ref[slot, 3 * KV_WIDTH:4 * KV_WIDTH, :].astype(BF16)) + _dot_nt(e_n, part(newT, 3))) * inv

    winT = winT_ref[...].astype(BF16)
    w_buf = winT.shape[1]
    k_pos = p_len - w_buf + lax.broadcasted_iota(jnp.int32, (n_rows, w_buf), 1)
    wmask = (p_len + row_tok - k_pos < WINDOW) & (k_pos >= 0)
    e_w, e_n, inv = _softmax_two(_dot(q_rows, winT[:KV_WIDTH]), wmask, _dot(q_rows, part(newT, 4)), new_ok)
    o_w = (_dot_nt(e_w, winT[KV_WIDTH:]) + _dot_nt(e_n, part(newT, 5))) * inv

    o_ref[...] = gate_ref[0] * o_c[:n_rows] + gate_ref[1] * o_s + gate_ref[2] * o_w


def _nsa_sample(q, gT, kv_new, cacheT, page_table, winT, kw, vw, kbias, vbias, dec_batch, dec_seq):
    n_pages = page_table.shape[1]
    page = cacheT.shape[2]
    p_len = n_pages * page
    n_chunks = p_len // CMP_STRIDE
    n_sel = -(-(p_len + dec_seq) // SEL_BLOCK)
    sel_rows = -(-n_sel // LANES) * LANES
    n_rows = NSA_HEADS * dec_seq
    q5 = q.reshape(dec_batch, dec_seq, NSA_KV_HEADS, NSA_GROUP, HEAD_DIM).transpose(0, 2, 3, 1, 4)
    eye = jnp.eye(NSA_KV_HEADS, dtype=F32)
    qrows = (q5[:, :, :, :, None, :] * eye[None, :, None, None, :, None]).reshape(dec_batch, n_rows, KV_WIDTH)
    qrows = jnp.pad(qrows, ((0, 0), (0, LANES - n_rows), (0, 0)))
    g4 = gT[:3 * NSA_HEADS].reshape(NSA_HEADS, 3, dec_batch, dec_seq).transpose(2, 1, 0, 3).reshape(dec_batch, 3, n_rows)
    gexp = jnp.broadcast_to(g4[..., None], (dec_batch, 3, n_rows, LANES))
    newT = jnp.pad(kv_new.reshape(dec_batch, dec_seq, KV6).transpose(0, 2, 1), ((0, 0), (0, 0), (0, LANES - dec_seq)))
    ovT = _overlap_T(sel_rows, n_chunks, n_sel, n_chunks - 1)
    fold = np.zeros((LANES, LANES), np.float32)
    for g in range(NSA_KV_HEADS):
        for r in range(NSA_GROUP):
            for s in range(dec_seq):
                fold[(g * NSA_GROUP + r) * dec_seq + s, g * dec_seq + s] = 1.0
    fold = jnp.asarray(fold)
    expand = jnp.asarray(np.arange(sel_rows)[:, None] == (np.arange(p_len) // SEL_BLOCK)[None, :]).astype(BF16)
    kwbd, kw2, _, _, _ = kw
    vwbd, vw2, _, _, _ = vw
    full = lambda a: pl.BlockSpec(a.shape, lambda b, pt: (0,) * a.ndim)
    per_seq = lambda a: pl.BlockSpec((None,) + a.shape[1:], lambda b, pt: (b,) + (0,) * (a.ndim - 1))
    hbm = pl.BlockSpec(memory_space=pl.ANY)
    grid_spec = pltpu.PrefetchScalarGridSpec(
        num_scalar_prefetch=1,
        grid=(dec_batch,),
        in_specs=[per_seq(qrows), per_seq(gexp), per_seq(newT), per_seq(winT), hbm, hbm,
                  full(kwbd), full(kw2), full(kbias), full(vwbd), full(vw2), full(vbias), full(ovT), full(fold)],
        out_specs=pl.BlockSpec((None, n_rows, KV_WIDTH), lambda b, pt: (b, 0, 0)),
        scratch_shapes=[pltpu.VMEM((2, CACHE_CHANNELS, p_len), F32),
                        pltpu.VMEM((p_len, KV_WIDTH), F32),
                        pltpu.VMEM((sel_rows, p_len), BF16),
                        pltpu.SemaphoreType.DMA((2,)),
                        pltpu.SemaphoreType.DMA((1,))])
    o = pl.pallas_call(
        functools.partial(_nsa_sample_kernel, dec_seq, n_sel),
        grid_spec=grid_spec,
        out_shape=jax.ShapeDtypeStruct((dec_batch, n_rows, KV_WIDTH), F32),
        compiler_params=_cparams(("arbitrary",)),
        name="nsa_sample",
    )(page_table, qrows, gexp, newT, winT, cacheT, expand, kwbd, kw2, kbias, vwbd, vw2, vbias, ovT, fold)
    o = o.reshape(dec_batch, NSA_KV_HEADS, NSA_GROUP, dec_seq, NSA_KV_HEADS, HEAD_DIM)
    o = jnp.stack([o[:, g, :, :, g] for g in range(NSA_KV_HEADS)], axis=1)
    return o.transpose(0, 3, 1, 2, 4).reshape(dec_batch * dec_seq, NSA_WIDTH)


def _head_ones(width):
    h = np.arange(width) // HEAD_DIM
    return (h[:, None] == h[None, :]).astype(np.float32)


def _rwkv_pre_kernel(seq, use_carry, rw_ref, shift_ref, mu_ref, w0_ref, a0_ref, kk_ref, ka_ref, rk_ref,
                     w2_ref, a2_ref, g2_ref, ones_ref,
                     r_ref, w_ref, k_ref, kkn_ref, b_ref, v_ref, g_ref, bonus_ref, carry_ref):
    i = pl.program_id(0)
    x = rw_ref[...]
    tm = x.shape[0]
    row = lax.broadcasted_iota(jnp.int32, (tm, 1), 0)
    prev = pltpu.roll(x, 1, 0)
    if use_carry:
        @pl.when(i == 0)
        def _():
            carry_ref[...] = jnp.zeros(carry_ref.shape, F32)
        prev = jnp.where(row == 0, carry_ref[...], prev)
        carry_ref[...] = x[tm - 1:tm]
    prev = jnp.where(((i * tm + row) & (seq - 1)) == 0, shift_ref[...], prev)
    xm = x + (prev - x) * mu_ref[...]
    wd = RWKV_WIDTH
    r, k, v = xm[:, :wd], xm[:, wd:2 * wd], xm[:, 2 * wd:3 * wd]
    lwa = xm[:, 3 * wd:3 * wd + DECAY_LORA + ICLR_LORA]
    lg = xm[:, 3 * wd + DECAY_LORA + ICLR_LORA:]
    w = -jax.nn.softplus(-(w0_ref[...] + _dot_hi(jnp.tanh(lwa), w2_ref[...]))) - 0.5
    a = jax.nn.sigmoid(a0_ref[...] + _dot_hi(lwa, a2_ref[...]))
    ones = ones_ref[...]
    kk = k * kk_ref[...]
    kk = kk / jnp.maximum(jnp.sqrt(_dot_hi(kk * kk, ones)), 1e-12)
    k2 = k * (1.0 + (a - 1.0) * ka_ref[...])
    r_ref[...] = r
    w_ref[...] = jnp.exp(-jnp.exp(w))
    k_ref[...] = k2
    kkn_ref[...] = kk
    b_ref[...] = kk * a
    v_ref[...] = v
    g_ref[...] = _dot_hi(jax.nn.sigmoid(lg), g2_ref[...])
    bonus_ref[...] = _dot_hi(r * k2 * rk_ref[...], ones) * v


def _rwkv_pre(rw, shift, seq, tm, rwkv_w):
    mu, w0, w2, a0, a2, g2, k_k, k_a, r_k, _, _ = rwkv_w
    n = rw.shape[0]
    use_carry = seq >= tm
    row1 = lambda a: a.reshape(1, -1)
    z = jnp.zeros((DECAY_LORA, RWKV_WIDTH), F32)
    w2p = jnp.concatenate([w2, z], axis=0)
    a2p = jnp.concatenate([z, a2], axis=0)
    ones = jnp.asarray(_head_ones(RWKV_WIDTH))
    consts = [row1(mu), row1(w0), row1(a0), row1(k_k), row1(k_a), row1(r_k), w2p, a2p, g2, ones]
    full = lambda a: pl.BlockSpec(a.shape, lambda i: (0,) * a.ndim)
    if use_carry:
        shift_spec = pl.BlockSpec((None, 1, RWKV_PROJ), lambda i: (i // (seq // tm), 0, 0))
    else:
        shift_spec = pl.BlockSpec((tm, RWKV_PROJ), lambda i: (i, 0))
    tile = pl.BlockSpec((tm, RWKV_WIDTH), lambda i: (i, 0))
    return pl.pallas_call(
        functools.partial(_rwkv_pre_kernel, seq, use_carry),
        grid=(n // tm,),
        in_specs=[pl.BlockSpec((tm, RWKV_PROJ), lambda i: (i, 0)), shift_spec] + [full(c) for c in consts],
        out_specs=(tile,) * 8,
        out_shape=(jax.ShapeDtypeStruct((n, RWKV_WIDTH), F32),) * 8,
        scratch_shapes=[pltpu.VMEM((1, RWKV_PROJ), F32)],
        compiler_params=_cparams(("arbitrary",)),
        name="rwkv_pre",
    )(rw, shift, *consts)


SCAN_SEQS = 2
SCAN_PIECES = 2
SCAN_UNROLL = 4


def _split_bf16(x):
    pieces, rest = [], x
    for i in range(SCAN_PIECES):
        piece = rest.astype(BF16)
        pieces.append(piece)
        if i + 1 < SCAN_PIECES:
            rest = rest - piece.astype(F32)
    return pieces


def _dot_stacked(pieces, w):
    rows = pieces[0].shape[0]
    n = SCAN_PIECES
    out = _dot(jnp.concatenate([part for x in pieces for part in _split_bf16(x)], axis=0), w)
    sums = []
    for i in range(len(pieces)):
        acc = out[n * i * rows:(n * i + 1) * rows]
        for j in range(1, n):
            acc = acc + out[(n * i + j) * rows:(n * i + j + 1) * rows]
        sums.append(acc)
    return sums


def _head_sums(ps, ones_half):
    half = ps[0].shape[1] // 2
    out = _dot_stacked([p[:, :half] for p in ps] + [p[:, half:] for p in ps], ones_half)
    return [jnp.concatenate([out[i], out[len(ps) + i]], axis=1) for i in range(len(ps))]


def _scan_kernel(tc, r_ref, w_ref, k_ref, kk_ref, b_ref, vts_ref, s0_ref, ones_ref, wv_ref, wy_ref,
                 y_ref, s_ref, ypart_ref):
    tb = pl.program_id(1)
    n_blocks = vts_ref.shape[1]

    @pl.when(tb == 0)
    def _():
        s_ref[...] = s0_ref[...]

    ones_half = ones_ref[...]
    rows = (HEAD_DIM, RWKV_WIDTH)

    def block(blk, _):
        vts = [vts_ref[s, blk] for s in range(SCAN_SEQS)]

        def step(tp, _):
            t = blk * tc + tp
            bc = lambda ref, s: jnp.broadcast_to(ref[s, pl.ds(t, 1), :], rows)
            seqs = range(SCAN_SEQS)
            st = [s_ref[s] for s in seqs]
            sa = _head_sums([st[s] * bc(kk_ref, s) for s in seqs], ones_half)
            vcol = _dot_stacked(vts, wv_ref[tp])
            st = [st[s] * bc(w_ref, s) - sa[s] * bc(b_ref, s) + vcol[s] * bc(k_ref, s) for s in seqs]
            y = _dot_stacked([st[s] * bc(r_ref, s) for s in seqs], wy_ref[tp])
            for s in seqs:
                s_ref[s] = st[s]
                ypart_ref[s, tp] = y[s]
            return 0

        lax.fori_loop(0, tc, step, 0, unroll=SCAN_UNROLL)
        for s in range(SCAN_SEQS):
            acc = ypart_ref[s, 0]
            for tp in range(1, tc):
                acc = acc + ypart_ref[s, tp]
            y_ref[s, blk] = acc
        return 0

    lax.fori_loop(0, n_blocks, block, 0)


def _scan_consts(tc):
    wv = np.zeros((tc, LANES, RWKV_WIDTH), np.float32)
    for tp in range(tc):
        for h in range(RWKV_HEADS):
            wv[tp, tp * RWKV_HEADS + h, h * HEAD_DIM:(h + 1) * HEAD_DIM] = 1.0
    wy = np.transpose(wv, (0, 2, 1))
    return (jnp.asarray(_head_ones(RWKV_WIDTH // 2)).astype(BF16), jnp.asarray(wv).astype(BF16),
            jnp.asarray(wy).astype(BF16))


def _rwkv_scan(r, w, k, kk, b, v, s0, batch, seq, tt):
    tc = min(LANES // RWKV_HEADS, seq)
    nb = seq // tc
    to3 = lambda a: a.reshape(batch, seq, RWKV_WIDTH)
    vts = v.reshape(batch, nb, tc, RWKV_HEADS, HEAD_DIM).transpose(0, 1, 4, 2, 3).reshape(batch, nb, HEAD_DIM, tc * RWKV_HEADS)
    vts = jnp.pad(vts, ((0, 0), (0, 0), (0, 0), (0, LANES - tc * RWKV_HEADS)))
    s0l = s0.transpose(0, 2, 1, 3).reshape(batch, HEAD_DIM, RWKV_WIDTH)
    ones_half, wv, wy = _scan_consts(tc)
    full = lambda a: pl.BlockSpec(a.shape, lambda p, t: (0,) * a.ndim)
    seq_spec = pl.BlockSpec((SCAN_SEQS, tt, RWKV_WIDTH), lambda p, t: (p, t, 0))
    blk_spec = pl.BlockSpec((SCAN_SEQS, tt // tc, HEAD_DIM, LANES), lambda p, t: (p, t, 0, 0))
    st_spec = pl.BlockSpec((SCAN_SEQS, HEAD_DIM, RWKV_WIDTH), lambda p, t: (p, 0, 0))
    y4, s_out = pl.pallas_call(
        functools.partial(_scan_kernel, tc),
        grid=(batch // SCAN_SEQS, seq // tt),
        in_specs=[seq_spec] * 5 + [blk_spec, st_spec, full(ones_half), full(wv), full(wy)],
        out_specs=(blk_spec, st_spec),
        out_shape=(jax.ShapeDtypeStruct((batch, nb, HEAD_DIM, LANES), F32),
                   jax.ShapeDtypeStruct((batch, HEAD_DIM, RWKV_WIDTH), F32)),
        scratch_shapes=[pltpu.VMEM((SCAN_SEQS, tc, HEAD_DIM, LANES), F32)],
        compiler_params=_cparams(("arbitrary", "arbitrary")),
        name="rwkv_scan",
    )(to3(r), to3(w), to3(k), to3(kk), to3(b), vts, s0l, ones_half, wv, wy)
    y = y4[..., :tc * RWKV_HEADS].reshape(batch, nb, HEAD_DIM, tc, RWKV_HEADS).transpose(0, 1, 3, 4, 2)
    s_fin = s_out.reshape(batch, HEAD_DIM, RWKV_HEADS, HEAD_DIM).transpose(0, 2, 1, 3)
    return y.reshape(batch * seq, RWKV_WIDTH), s_fin


def _out_kernel(x_ref, nsa_ref, y_ref, bonus_ref, g_ref, gt_ref, lnw_ref, lnb_ref, ones_ref, wo_ref, gp_ref, o_ref):
    y = y_ref[...]
    ones = ones_ref[...]
    inv = 1.0 / HEAD_DIM
    yc = y - _dot_hi(y, ones) * inv
    var = _dot_hi(yc * yc, ones) * inv
    yn = yc * lax.rsqrt(var + GN_EPS) * lnw_ref[...] + lnb_ref[...]
    mr = ((yn + bonus_ref[...]) * g_ref[...]).astype(BF16)
    z = _dot(nsa_ref[...].astype(BF16), wo_ref[:NSA_WIDTH, :]) + _dot(mr, wo_ref[NSA_WIDTH:, :])
    zn = z * lax.rsqrt(jnp.mean(z * z, axis=-1, keepdims=True) + NORM_EPS) * gp_ref[...]
    o_ref[...] = x_ref[...] + gt_ref[...] * zn


def _mixer_out(x, nsa, y, bonus, g, mod, tokens_per_row, ln_w, ln_b, w_out, g_post, tm):
    n = x.shape[0]
    ones = jnp.asarray(_head_ones(RWKV_WIDTH))
    wo = w_out.astype(BF16)
    row1 = lambda a: a.reshape(1, -1)
    full = lambda a: pl.BlockSpec(a.shape, lambda i: (0,) * a.ndim)
    half = pl.BlockSpec((tm, RWKV_WIDTH), lambda i: (i, 0))
    xspec = pl.BlockSpec((tm, D_MODEL), lambda i: (i, 0))
    lw, lb, gp = row1(ln_w), row1(ln_b), row1(g_post)
    return pl.pallas_call(
        _out_kernel,
        grid=(n // tm,),
        in_specs=[xspec, half, half, half, half, _mod_specs(mod, tm, tokens_per_row, 2),
                  full(lw), full(lb), full(ones), full(wo), full(gp)],
        out_specs=xspec,
        out_shape=jax.ShapeDtypeStruct((n, D_MODEL), F32),
        compiler_params=_cparams(("arbitrary",)),
        name="mixer_out",
    )(x, nsa, y, bonus, g, mod, lw, lb, ones, wo, gp)


def _staircase():
    return [(a, b) for a in range(PEER_TOPK) for b in range(PEER_TOPK) if (a + 1) * (b + 1) <= PEER_TOPK]


def _top16_ranked(s):
    rows = s.shape[0]
    idx_row = lax.broadcasted_iota(jnp.int32, s.shape, 0)
    rank = jnp.full(s.shape, 99.0, F32)
    vals = []
    for it in range(PEER_TOPK):
        m = jnp.max(s, axis=0, keepdims=True)
        idx = jnp.min(jnp.where(s == m, idx_row, rows), axis=0, keepdims=True)
        hit = idx_row == idx
        rank = jnp.where(hit, np.float32(it), rank)
        s = jnp.where(hit, BELOW_NEG_INF, s)
        vals.append(m)
    return rank, vals


PEER_STRIP = 128
PEER_EXPERT_BLOCK = 1024
PEER_CHAIN = 256


def _peer_route(hT, wqT_ref, keys_ref, hd):
    n = hT.shape[1]
    stairs = _staircase()
    n_cand = -(-len(stairs) // SUBLANES) * SUBLANES
    cand_row = lax.broadcasted_iota(jnp.int32, (n_cand, n), 0)
    ranks, vals = [], []
    for c in range(2):
        q = _dot(wqT_ref[(hd * 2 + c) * PEER_HALF:(hd * 2 + c + 1) * PEER_HALF, :], hT)
        q = q * lax.rsqrt(jnp.mean(q * q, axis=0, keepdims=True) + NORM_EPS)
        s = _dot_hi(keys_ref[hd * 2 + c], q)
        rank, top = _top16_ranked(s)
        ranks.append((rank, s))
        vals.append(top)
    cand = jnp.concatenate([vals[0][a] + vals[1][b] for a, b in stairs]
                           + [jnp.full((n_cand - len(stairs), n), BELOW_NEG_INF, F32)], axis=0)
    chosen = jnp.zeros(cand.shape, F32)
    work = cand
    for _ in range(PEER_TOPK):
        m = jnp.max(work, axis=0, keepdims=True)
        idx = jnp.min(jnp.where(work == m, cand_row, n_cand), axis=0, keepdims=True)
        hit = cand_row == idx
        chosen = jnp.where(hit, 1.0, chosen)
        work = jnp.where(hit, BELOW_NEG_INF, work)
    top_sum = vals[0][0] + vals[1][0]
    z = jnp.sum(chosen * jnp.exp(jnp.where(chosen > 0.5, cand, top_sum) - top_sum), axis=0, keepdims=True)
    rank1, s1 = ranks[0]
    rank2, s2 = ranks[1]
    cnt = jnp.zeros(rank1.shape, F32)
    for a in range(PEER_TOPK):
        rows_a = [r for r, (ka, _) in enumerate(stairs) if ka == a]
        c_a = chosen[rows_a[0]:rows_a[0] + 1]
        for r in rows_a[1:]:
            c_a = c_a + chosen[r:r + 1]
        cnt = jnp.where(rank1 == np.float32(a), c_a, cnt)
    return jnp.exp(s1 - vals[0][0]) / z, cnt, jnp.exp(s2 - vals[1][0]), rank2


def _peer_kernel(x_ref, g2_ref, sc_ref, sh_ref, gt_ref, gp_ref, wqT_ref, keys_ref, u_ref, vT_ref,
                 o_ref, hT_ref, a_ref, cnt_ref, b_ref, r2_ref, acc_ref):
    i = pl.program_id(1)
    n = x_ref.shape[0]
    strip = min(PEER_STRIP, n)

    @pl.when(i == 0)
    def _():
        x = x_ref[...]
        h = x * lax.rsqrt(jnp.mean(x * x, axis=-1, keepdims=True) + NORM_EPS) * g2_ref[...]
        hT_ref[...] = (h * (1.0 + sc_ref[...]) + sh_ref[...]).T.astype(BF16)
        acc_ref[...] = jnp.zeros(acc_ref.shape, F32)
        for st in range(n // strip):
            lanes = slice(st * strip, (st + 1) * strip)
            hT = hT_ref[:, lanes]
            for hd in range(PEER_HEADS):
                a, cnt, bcol, r2 = _peer_route(hT, wqT_ref, keys_ref, hd)
                a_ref[hd, :, lanes] = a
                cnt_ref[hd, :, lanes] = cnt
                b_ref[hd, :, lanes] = bcol
                r2_ref[hd, :, lanes] = r2

    hT = hT_ref[...]
    per_chain = PEER_CHAIN // PEER_KEYS
    total = None
    for c in range(u_ref.shape[0] // PEER_CHAIN):
        experts = slice(c * PEER_CHAIN, (c + 1) * PEER_CHAIN)
        act = _gelu(_dot(u_ref[experts, :], hT))
        weighted = []
        for j in range(per_chain):
            blk = (i * (u_ref.shape[0] // PEER_KEYS)) + c * per_chain + j
            wgt = jnp.zeros((PEER_KEYS, n), F32)
            for hd in range(PEER_HEADS):
                a_row = a_ref[hd, pl.ds(blk, 1), :]
                c_row = cnt_ref[hd, pl.ds(blk, 1), :]
                wgt = wgt + jnp.where(r2_ref[hd] < c_row, b_ref[hd] * a_row, 0.0)
            weighted.append((wgt * act[j * PEER_KEYS:(j + 1) * PEER_KEYS]).astype(BF16))
        part = _dot(vT_ref[:, experts], jnp.concatenate(weighted, axis=0))
        total = part if total is None else total + part
    acc_ref[...] += total

    @pl.when(i == pl.num_programs(1) - 1)
    def _():
        z = acc_ref[...].T
        zn = z * lax.rsqrt(jnp.mean(z * z, axis=-1, keepdims=True) + NORM_EPS) * gp_ref[...]
        o_ref[...] = x_ref[...] + gt_ref[...] * zn


def _peer_layer(x, g_pre, mod, tokens_per_row, g_post, w_q, sub_keys, u_bf, vT_bf, tm):
    n = x.shape[0]
    wqT = w_q.astype(BF16).T
    keys = sub_keys.reshape(PEER_HEADS * 2, PEER_KEYS, PEER_HALF)
    eb = PEER_EXPERT_BLOCK
    row1 = lambda a: a.reshape(1, -1)
    g2, gp = row1(g_pre), row1(g_post)
    full = lambda a: pl.BlockSpec(a.shape, lambda t, i: (0,) * a.ndim)
    modspec = lambda col: _mod_specs(mod, tm, tokens_per_row, col)
    xspec = pl.BlockSpec((tm, D_MODEL), lambda t, i: (t, 0))
    per_head = pltpu.VMEM((PEER_HEADS, PEER_KEYS, tm), F32)
    return pl.pallas_call(
        _peer_kernel,
        grid=(n // tm, u_bf.shape[0] // eb),
        in_specs=[xspec, full(g2), modspec(4), modspec(3), modspec(5), full(gp), full(wqT), full(keys),
                  pl.BlockSpec((eb, D_MODEL), lambda t, i: (i, 0)),
                  pl.BlockSpec((D_MODEL, eb), lambda t, i: (0, i))],
        out_specs=xspec,
        out_shape=jax.ShapeDtypeStruct((n, D_MODEL), F32),
        scratch_shapes=[pltpu.VMEM((D_MODEL, tm), BF16), per_head, per_head, per_head, per_head,
                        pltpu.VMEM((D_MODEL, tm), F32)],
        compiler_params=_cparams(("arbitrary", "arbitrary")),
        name="peer",
    )(x, g2, mod, mod, mod, gp, wqT, keys, u_bf, vT_bf)


def kernel(x_prompt, x_sample, c_prompt, c_sample, cache_nsa, page_table, cache_win, state_wkv, state_shift, w_ada, b_ada, norm_pre1, norm_post1, norm_pre2, norm_post2, w_in, w_out, cmp_k_w1, cmp_k_w2, cmp_k_pe, cmp_v_w1, cmp_v_w2, cmp_v_pe, rwkv_mu, rwkv_w0, rwkv_w2, rwkv_a0, rwkv_a2, rwkv_g2, rwkv_k_k, rwkv_k_a, rwkv_r_k, rwkv_ln_w, rwkv_ln_b, peer_w_q, peer_sub_keys, peer_u, peer_v):
    batch, seq, d = x_prompt.shape
    dec_batch, dec_seq, _ = x_sample.shape
    depth = w_ada.shape[0]
    assert d == D_MODEL and seq % WINDOW == 0 and seq & (seq - 1) == 0 and dec_seq & (dec_seq - 1) == 0
    assert batch % SCAN_SEQS == 0 and dec_batch % SCAN_SEQS == 0
    n_p, n_s = batch * seq, dec_batch * dec_seq
    tile = lambda n: min(256, n)
    xp = x_prompt.reshape(n_p, d)
    xs = x_sample.reshape(n_s, d)
    c_all = jnp.concatenate([c_prompt, c_sample], axis=0)
    c_all = jnp.pad(c_all, ((0, -c_all.shape[0] % SUBLANES), (0, 0)))
    w_keep = min(WINDOW, seq)
    outs = [[] for _ in range(8)]
    for l in range(depth):
        mod = _modulation(c_all, w_ada[l], b_ada[l])
        mod_p = mod[:batch].reshape(batch, 1, 6 * d)
        mod_s = jnp.repeat(mod[batch:batch + dec_batch], dec_seq, axis=0)
        kw = _compress_weights(cmp_k_w1[l], cmp_k_w2[l], cmp_k_pe[l])
        vw = _compress_weights(cmp_v_w1[l], cmp_v_w2[l], cmp_v_pe[l])
        rwkv_w = (rwkv_mu[l], rwkv_w0[l], rwkv_w2[l], rwkv_a0[l], rwkv_a2[l], rwkv_g2[l],
                  rwkv_k_k[l], rwkv_k_a[l], rwkv_r_k[l], rwkv_ln_w[l], rwkv_ln_b[l])
        u_bf = peer_u[l].astype(BF16)
        vT_bf = peer_v[l].astype(BF16).T

        qT, kv, kvb, kvT, gT, rw = _in_projection(xp, norm_pre1[l], mod_p, seq, w_in[l], tile(n_p))
        kc, vcT, kbias, vbias = _compress_prompt(kv, batch, seq, kw, vw)
        nsa = _nsa_prompt(qT, kvb, kvT, gT, kc, vcT, batch, seq)
        r, w, k, kk, b, v, g, bonus = _rwkv_pre(rw, jnp.zeros((batch, 1, RWKV_PROJ), F32), seq, tile(n_p), rwkv_w)
        y, wkv_p = _rwkv_scan(r, w, k, kk, b, v, jnp.zeros((batch, RWKV_HEADS, HEAD_DIM, HEAD_DIM), F32),
                              batch, seq, min(256, seq))
        x1 = _mixer_out(xp, nsa, y, bonus, g, mod_p, seq, rwkv_ln_w[l], rwkv_ln_b[l], w_out[l], norm_post1[l], tile(n_p))
        xp = _peer_layer(x1, norm_pre2[l], mod_p, seq, norm_post2[l], peer_w_q[l], peer_sub_keys[l], u_bf, vT_bf, tile(n_p))
        kv3 = kv.reshape(batch, seq, KV6)
        outs[0].append(kv3[:, :, :4 * KV_WIDTH].reshape(batch, seq, 4, NSA_KV_HEADS, HEAD_DIM))
        outs[2].append(kv3[:, seq - w_keep:, 4 * KV_WIDTH:].reshape(batch, w_keep, 2, NSA_KV_HEADS, HEAD_DIM))
        outs[4].append(wkv_p)
        outs[6].append(rw.reshape(batch, seq, RWKV_PROJ)[:, -1])

        qT, kv, _, _, gT, rw = _in_projection(xs, norm_pre1[l], mod_s, None, w_in[l], tile(n_s))
        cacheT = jnp.transpose(cache_nsa[l], (0, 2, 3, 4, 1)).reshape(cache_nsa.shape[1], CACHE_CHANNELS, cache_nsa.shape[2])
        winT = jnp.transpose(cache_win[l], (0, 2, 3, 4, 1)).reshape(dec_batch, 2 * KV_WIDTH, cache_win.shape[2])
        nsa = _nsa_sample(qT.T, gT, kv, cacheT, page_table, winT, kw, vw, kbias, vbias, dec_batch, dec_seq)
        shift_tok = jnp.repeat(state_shift[l], dec_seq, axis=0)
        r, w, k, kk, b, v, g, bonus = _rwkv_pre(rw, shift_tok, dec_seq, tile(n_s), rwkv_w)
        y, wkv_s = _rwkv_scan(r, w, k, kk, b, v, state_wkv[l], dec_batch, dec_seq, dec_seq)
        x1 = _mixer_out(xs, nsa, y, bonus, g, mod_s, None, rwkv_ln_w[l], rwkv_ln_b[l], w_out[l], norm_post1[l], tile(n_s))
        xs = _peer_layer(x1, norm_pre2[l], mod_s, None, norm_post2[l], peer_w_q[l], peer_sub_keys[l], u_bf, vT_bf, tile(n_s))
        kv3 = kv.reshape(dec_batch, dec_seq, KV6)
        outs[1].append(kv3[:, :, :4 * KV_WIDTH].reshape(dec_batch, dec_seq, 4, NSA_KV_HEADS, HEAD_DIM))
        new_win = kv3[:, :, 4 * KV_WIDTH:].reshape(dec_batch, dec_seq, 2, NSA_KV_HEADS, HEAD_DIM)
        outs[3].append(jnp.concatenate([cache_win[l], new_win], axis=1)[:, dec_seq:])
        outs[5].append(wkv_s)
        outs[7].append(rw.reshape(dec_batch, dec_seq, RWKV_PROJ)[:, -1])
    return (xp.reshape(batch, seq, d), xs.reshape(dec_batch, dec_seq, d)) + tuple(jnp.stack(o) for o in outs)
```
